```python
import math
import jax, jax.numpy as jnp
from jax import lax
import numpy as np

D_MODEL = 1024
BATCH = 4
SEQ = 8192
DEPTH = 2
DEC_BATCH = 128
DEC_SEQ = 4
PAST_LEN = 16384
PAGE_SIZE = 128

N_A = DEPTH // 2
N_B = DEPTH - N_A
D_INNER = 2 * D_MODEL
SSM_HEADDIM = 64
SSM_HEADS = D_INNER // SSM_HEADDIM
SSM_GROUPS = 4
SSM_STATE = 128
CONV_W = 4
CONV_DIM = D_INNER + 2 * SSM_GROUPS * SSM_STATE
SSD_CHUNK = 128
MLA_HEADS = 16
Q_LORA = 384
KV_LORA = 256
NOPE_DIM = 64
ROPE_DIM = 32
V_DIM = 64
ROPE_THETA = 10000.0
Q_BLOCK = 128
ATTN_SCALE = (NOPE_DIM + ROPE_DIM) ** -0.5
PEER_HEADS = 8
PEER_KEYS = 128
PEER_EXPERTS = PEER_KEYS * PEER_KEYS
PEER_QDIM = 256
PEER_TOPK = 16
PEER_TOKEN_BLOCK = 512
PLE_DIM = 256
EPS = 1e-6

kernel_name = 'yoco_ssd_mla_peer_step'


def rms_norm(x, g):
    xf = x.astype(jnp.float32)
    y = xf * lax.rsqrt(jnp.mean(xf * xf, -1, keepdims=True) + EPS)
    return (y * g.astype(jnp.float32)).astype(x.dtype)


def rope(x, pos):
    half = ROPE_DIM // 2
    inv = ROPE_THETA ** (-jnp.arange(half, dtype=jnp.float32) / half)
    ang = pos.astype(jnp.float32)[:, None] * inv
    ang = ang.reshape((1, ang.shape[0]) + (1,) * (x.ndim - 3) + (half,))
    cos, sin = jnp.cos(ang), jnp.sin(ang)
    xf = x.astype(jnp.float32)
    x1, x2 = xf[..., :half], xf[..., half:]
    return jnp.concatenate([x1 * cos - x2 * sin, x1 * sin + x2 * cos], -1).astype(x.dtype)


def causal_dwconv(u, prev, w, b):
    full = jnp.concatenate([prev, u], 1)
    y = lax.conv_general_dilated(full, w.astype(full.dtype)[:, None, :], (1,), 'VALID',
                                 dimension_numbers=('NWC', 'WIO', 'NWC'),
                                 feature_group_count=u.shape[-1])
    return y + b, full[:, -(CONV_W - 1):]


def segsum(a):
    cs = jnp.cumsum(a, -1)
    q = a.shape[-1]
    mask = jnp.tril(jnp.ones((q, q), bool))
    return jnp.where(mask, cs[..., :, None] - cs[..., None, :], -jnp.inf)


def ssd(x, dt, A, Bm, Cm, h0):
    b, l = x.shape[:2]
    q = SSD_CHUNK if l % SSD_CHUNK == 0 else l
    c = l // q
    G, R, P, N = SSM_GROUPS, SSM_HEADS // SSM_GROUPS, SSM_HEADDIM, SSM_STATE
    xd = (x * dt[..., None]).reshape(b, c, q, G, R, P)
    a = (dt * A).reshape(b, c, q, G, R).transpose(0, 3, 4, 1, 2)
    a_cum = jnp.cumsum(a, -1)
    Bc = Bm.reshape(b, c, q, G, N)
    Cc = Cm.reshape(b, c, q, G, N)
    Lmat = jnp.exp(segsum(a))
    cb = jnp.einsum('bclgn,bcsgn->bgcls', Cc, Bc)
    y_diag = jnp.einsum('bgcls,bgrcls,bcsgrp->bclgrp', cb, Lmat, xd)
    decay = jnp.exp(a_cum[..., -1:] - a_cum)
    st = jnp.einsum('bclgn,bgrcl,bclgrp->bcgrpn', Bc, decay, xd)
    chunk_decay = jnp.exp(a_cum[..., -1])

    def step(h, inp):
        s_c, d_c = inp
        return h * d_c[..., None, None] + s_c, h

    h_last, h_in = lax.scan(step, h0.reshape(b, G, R, P, N),
                            (st.transpose(1, 0, 2, 3, 4, 5), chunk_decay.transpose(3, 0, 1, 2)))
    y_off = jnp.einsum('bclgn,cbgrpn,bgrcl->bclgrp', Cc, h_in, jnp.exp(a_cum))
    y = (y_diag + y_off).reshape(b, l, SSM_HEADS, P)
    return y, h_last.reshape(b, SSM_HEADS, P, N)


def mamba_mixer(u, conv_prev, h0, w_in, conv_w, conv_b, dt_bias, a_log, d_skip, g_norm, w_out):
    b, l, _ = u.shape
    f32 = jnp.float32
    z, xbc, dt = jnp.split(u @ w_in, [D_INNER, D_INNER + CONV_DIM], -1)
    xbc, conv_new = causal_dwconv(xbc, conv_prev, conv_w, conv_b)
    xbc = jax.nn.silu(xbc)
    xs, Bm, Cm = jnp.split(xbc, [D_INNER, D_INNER + SSM_GROUPS * SSM_STATE], -1)
    dt = jax.nn.softplus(dt.astype(f32) + dt_bias.astype(f32))
    A = -jnp.exp(a_log.astype(f32))
    xs = xs.astype(f32).reshape(b, l, SSM_HEADS, SSM_HEADDIM)
    y, h_new = ssd(xs, dt, A,
                   Bm.astype(f32).reshape(b, l, SSM_GROUPS, SSM_STATE),
                   Cm.astype(f32).reshape(b, l, SSM_GROUPS, SSM_STATE),
                   h0.astype(f32))
    y = y + d_skip.astype(f32)[:, None] * xs
    y = y.reshape(b, l, D_INNER) * jax.nn.silu(z.astype(f32))
    y = rms_norm(y.reshape(b, l, SSM_GROUPS, D_INNER // SSM_GROUPS),
                 g_norm.reshape(SSM_GROUPS, -1)).reshape(b, l, D_INNER)
    return y.astype(u.dtype) @ w_out, conv_new, h_new.astype(u.dtype)


def mla_latent(h, pos, g_in, w_dkv, g_ckv, w_kr, g_kr):
    hn = rms_norm(h, g_in)
    ckv = rms_norm(hn @ w_dkv, g_ckv)
    kr = rope(rms_norm(hn @ w_kr, g_kr), pos)
    return ckv, kr


def mla_queries(u, pos, w_dq, g_cq, w_uq, g_qn, g_qr):
    b, l, _ = u.shape
    cq = rms_norm(u @ w_dq, g_cq)
    q = (cq @ w_uq).reshape(b, l, MLA_HEADS, NOPE_DIM + ROPE_DIM)
    qn = rms_norm(q[..., :NOPE_DIM], g_qn)
    qr = rope(rms_norm(q[..., NOPE_DIM:], g_qr), pos)
    return qn, qr


def expand_keys(ckv, w_uk, g_kn):
    return rms_norm((ckv @ w_uk).reshape(ckv.shape[:2] + (MLA_HEADS, NOPE_DIM)), g_kn)


def mla_attend_prompt(qn, qr, ckv, kr, w_uk, g_kn, w_uv):
    b, l = qn.shape[:2]
    kn = expand_keys(ckv, w_uk, g_kn)
    v = (ckv @ w_uv).reshape(b, l, MLA_HEADS, V_DIM)
    nblk = l // Q_BLOCK
    kpos = jnp.arange(l)

    def block(args):
        qn_b, qr_b, start = args
        s = jnp.einsum('bqhd,bkhd->bhqk', qn_b, kn, preferred_element_type=jnp.float32)
        s = s + jnp.einsum('bqhd,bkd->bhqk', qr_b, kr, preferred_element_type=jnp.float32)
        qpos = start + jnp.arange(Q_BLOCK)
        s = jnp.where(kpos[None, :] <= qpos[:, None], s * ATTN_SCALE, -jnp.inf)
        p = jax.nn.softmax(s, -1)
        return jnp.einsum('bhqk,bkhd->bqhd', p.astype(v.dtype), v)

    qn_blk = qn.reshape(b, nblk, Q_BLOCK, MLA_HEADS, NOPE_DIM).transpose(1, 0, 2, 3, 4)
    qr_blk = qr.reshape(b, nblk, Q_BLOCK, MLA_HEADS, ROPE_DIM).transpose(1, 0, 2, 3, 4)
    out = lax.map(block, (qn_blk, qr_blk, jnp.arange(nblk) * Q_BLOCK))
    return out.transpose(1, 0, 2, 3, 4).reshape(b, l, MLA_HEADS * V_DIM)


def mla_attend_sample(qn, qr, ckv_new, kr_new, cache_ckv, cache_krope, page_table, w_uk, g_kn, w_uv):
    b, lq = qn.shape[:2]
    f32 = jnp.float32

    def scores(c_blk, kr_blk):
        kn = expand_keys(c_blk, w_uk, g_kn)
        s = jnp.einsum('bqhd,bkhd->bhqk', qn, kn, preferred_element_type=f32)
        s = s + jnp.einsum('bqhd,bkd->bhqk', qr, kr_blk, preferred_element_type=f32)
        return s * ATTN_SCALE

    def update(carry, s, c_blk):
        m, lsum, acc = carry
        m_new = jnp.maximum(m, s.max(-1))
        alpha = jnp.exp(m - m_new)
        p = jnp.exp(s - m_new[..., None])
        acc = acc * alpha[..., None] + jnp.einsum('bhqk,bkc->bhqc', p, c_blk.astype(f32))
        return (m_new, lsum * alpha + p.sum(-1), acc)

    def page_step(carry, pages):
        c_blk = cache_ckv[pages]
        kr_blk = cache_krope[pages]
        return update(carry, scores(c_blk, kr_blk), c_blk), None

    init = (jnp.full((b, MLA_HEADS, lq), -jnp.inf, f32),
            jnp.zeros((b, MLA_HEADS, lq), f32),
            jnp.zeros((b, MLA_HEADS, lq, KV_LORA), f32))
    carry, _ = lax.scan(page_step, init, page_table.T)
    causal = jnp.tril(jnp.ones((lq, lq), bool))
    s_new = jnp.where(causal, scores(ckv_new, kr_new), -jnp.inf)
    _, lsum, acc = update(carry, s_new, ckv_new)
    o_lat = acc / lsum[..., None]
    o = jnp.einsum('bhqc,chd->bqhd', o_lat, w_uv.astype(f32).reshape(KV_LORA, MLA_HEADS, V_DIM))
    return o.reshape(b, lq, MLA_HEADS * V_DIM).astype(qn.dtype)


def peer(u, w_q, g_q, sub_keys, u_tab, v_tab):
    b, l, d = u.shape
    t = b * l
    nblk = -(-t // PEER_TOKEN_BLOCK)
    flat = jnp.pad(u.reshape(t, d), ((0, nblk * PEER_TOKEN_BLOCK - t), (0, 0)))
    half = PEER_QDIM // 2

    def block(xb):
        q = rms_norm((xb @ w_q).reshape(-1, PEER_HEADS, PEER_QDIM), g_q)
        s1 = jnp.einsum('thd,kd->thk', q[..., :half], sub_keys[0], preferred_element_type=jnp.float32)
        s2 = jnp.einsum('thd,kd->thk', q[..., half:], sub_keys[1], preferred_element_type=jnp.float32)
        v1, i1 = lax.top_k(s1, PEER_TOPK)
        v2, i2 = lax.top_k(s2, PEER_TOPK)
        cand = (v1[..., :, None] + v2[..., None, :]).reshape(xb.shape[0], PEER_HEADS, PEER_TOPK * PEER_TOPK)
        vals, ci = lax.top_k(cand, PEER_TOPK)
        e1 = jnp.take_along_axis(i1, ci // PEER_TOPK, -1)
        e2 = jnp.take_along_axis(i2, ci % PEER_TOPK, -1)
        eidx = e1 * PEER_KEYS + e2
        gates = jax.nn.softmax(vals, -1)
        act = jax.nn.gelu(jnp.einsum('td,thkd->thk', xb, u_tab[eidx]), approximate=False)
        return jnp.einsum('thk,thkd->td', (gates * act).astype(xb.dtype), v_tab[eidx])

    out = lax.map(block, flat.reshape(nblk, PEER_TOKEN_BLOCK, d))
    return out.reshape(-1, d)[:t].reshape(b, l, d)


def ple(h, p, g, w_proj, w_gate):
    return (p @ w_proj) * jax.nn.sigmoid(rms_norm(h, g) @ w_gate)


def setup_inputs(seed: int = 0) -> dict:
    key = jax.random.key(seed)
    keys = jax.random.split(key, 64)
    cnt = [0]
    f32 = jnp.float32

    def nk():
        cnt[0] += 1
        return keys[cnt[0] - 1]

    def nrm(shape, scale=1.0):
        return jax.random.normal(nk(), shape, f32) * scale

    def gain(shape):
        return 1.0 + 0.02 * jax.random.normal(nk(), shape, f32)

    n_pages = PAST_LEN // PAGE_SIZE
    n_used = DEC_BATCH * n_pages
    n_pool = n_used + n_used // 4
    page_table = jax.random.permutation(nk(), n_pool)[:n_used].reshape(DEC_BATCH, n_pages).astype(jnp.int32)

    x_prompt = nrm((BATCH, SEQ, D_MODEL))
    x_sample = nrm((DEC_BATCH, DEC_SEQ, D_MODEL))
    p_prompt = nrm((DEPTH, BATCH, SEQ, PLE_DIM))
    p_sample = nrm((DEPTH, DEC_BATCH, DEC_SEQ, PLE_DIM))
    state_ssm = nrm((N_A, DEC_BATCH, SSM_HEADS, SSM_HEADDIM, SSM_STATE), 0.5)
    state_conv = nrm((N_A, DEC_BATCH, CONV_W - 1, CONV_DIM))
    cache_ckv = nrm((n_pool, PAGE_SIZE, KV_LORA))
    cache_krope = nrm((n_pool, PAGE_SIZE, ROPE_DIM))

    u_dt = jax.random.uniform(nk(), (N_A, SSM_HEADS), f32)
    dt0 = jnp.exp(u_dt * (math.log(0.1) - math.log(1e-3)) + math.log(1e-3))
    m_dt_bias = dt0 + jnp.log(-jnp.expm1(-dt0))
    m_a_log = jnp.log(jax.random.uniform(nk(), (N_A, SSM_HEADS), f32, 1.0, 16.0))

    return {
        'x_prompt': x_prompt, 'x_sample': x_sample, 'p_prompt': p_prompt, 'p_sample': p_sample,
        'state_ssm': state_ssm, 'state_conv': state_conv,
        'cache_ckv': cache_ckv, 'cache_krope': cache_krope, 'page_table': page_table,
        'ln_mix': gain((DEPTH, D_MODEL)), 'ln_ffn': gain((DEPTH, D_MODEL)), 'ln_ple': gain((DEPTH, D_MODEL)),
        'm_w_in': nrm((N_A, D_MODEL, D_INNER + CONV_DIM + SSM_HEADS), D_MODEL ** -0.5),
        'm_conv_w': nrm((N_A, CONV_W, CONV_DIM), CONV_W ** -0.5),
        'm_conv_b': nrm((N_A, CONV_DIM), 0.02),
        'm_dt_bias': m_dt_bias, 'm_a_log': m_a_log,
        'm_d': gain((N_A, SSM_HEADS)),
        'm_g_norm': gain((N_A, D_INNER)),
        'm_w_out': nrm((N_A, D_INNER, D_MODEL), D_INNER ** -0.5),
        'kv_g_in': gain((D_MODEL,)),
        'kv_w_dkv': nrm((D_MODEL, KV_LORA), D_MODEL ** -0.5),
        'kv_g_ckv': gain((KV_LORA,)),
        'kv_w_kr': nrm((D_MODEL, ROPE_DIM), D_MODEL ** -0.5),
        'kv_g_kr': gain((ROPE_DIM,)),
        'kv_w_uk': nrm((KV_LORA, MLA_HEADS * NOPE_DIM), KV_LORA ** -0.5),
        'kv_g_kn': gain((NOPE_DIM,)),
        'kv_w_uv': nrm((KV_LORA, MLA_HEADS * V_DIM), KV_LORA ** -0.5),
        'q_w_dq': nrm((N_B, D_MODEL, Q_LORA), D_MODEL ** -0.5),
        'q_g_cq': gain((N_B, Q_LORA)),
        'q_w_uq': nrm((N_B, Q_LORA, MLA_HEADS * (NOPE_DIM + ROPE_DIM)), Q_LORA ** -0.5),
        'q_g_qn': gain((N_B, NOPE_DIM)),
        'q_g_qr': gain((N_B, ROPE_DIM)),
        'a_w_o': nrm((N_B, MLA_HEADS * V_DIM, D_MODEL), (MLA_HEADS * V_DIM) ** -0.5),
        'peer_w_q': nrm((DEPTH, D_MODEL, PEER_HEADS * PEER_QDIM), D_MODEL ** -0.5),
        'peer_g_q': gain((DEPTH, PEER_QDIM)),
        'peer_sub_keys': nrm((DEPTH, 2, PEER_KEYS, PEER_QDIM // 2), (PEER_QDIM // 2) ** -0.5),
        'peer_u': nrm((DEPTH, PEER_EXPERTS, D_MODEL), D_MODEL ** -0.5),
        'peer_v': nrm((DEPTH, PEER_EXPERTS, D_MODEL), PEER_HEADS ** -0.5),
        'ple_w_proj': nrm((DEPTH, PLE_DIM, D_MODEL), PLE_DIM ** -0.5),
        'ple_w_gate': nrm((DEPTH, D_MODEL, D_MODEL), D_MODEL ** -0.5),
    }


def reference(x_prompt, x_sample, p_prompt, p_sample, state_ssm, state_conv, cache_ckv, cache_krope, page_table,
              ln_mix, ln_ffn, ln_ple,
              m_w_in, m_conv_w, m_conv_b, m_dt_bias, m_a_log, m_d, m_g_norm, m_w_out,
              kv_g_in, kv_w_dkv, kv_g_ckv, kv_w_kr, kv_g_kr, kv_w_uk, kv_g_kn, kv_w_uv,
              q_w_dq, q_g_cq, q_w_uq, q_g_qn, q_g_qr, a_w_o,
              peer_w_q, peer_g_q, peer_sub_keys, peer_u, peer_v,
              ple_w_proj, ple_w_gate):
    bp, lp = x_prompt.shape[:2]
    pos_p = jnp.arange(lp)
    pos_s = page_table.shape[1] * PAGE_SIZE + jnp.arange(x_sample.shape[1])
    hp, hs = x_prompt, x_sample
    ssm_p, conv_p, ssm_s, conv_s = [], [], [], []
    for i in range(DEPTH):
        if i < N_A:
            mw = (m_w_in[i], m_conv_w[i], m_conv_b[i], m_dt_bias[i], m_a_log[i], m_d[i], m_g_norm[i], m_w_out[i])
            conv0 = jnp.zeros((bp, CONV_W - 1, CONV_DIM), hp.dtype)
            ssm0 = jnp.zeros((bp, SSM_HEADS, SSM_HEADDIM, SSM_STATE), hp.dtype)
            yp, c_p, s_p = mamba_mixer(rms_norm(hp, ln_mix[i]), conv0, ssm0, *mw)
            ys, c_s, s_s = mamba_mixer(rms_norm(hs, ln_mix[i]), state_conv[i], state_ssm[i], *mw)
            conv_p.append(c_p)
            ssm_p.append(s_p)
            conv_s.append(c_s)
            ssm_s.append(s_s)
        else:
            if i == N_A:
                kvw = (kv_g_in, kv_w_dkv, kv_g_ckv, kv_w_kr, kv_g_kr)
                ckv_p, kr_p = mla_latent(hp, pos_p, *kvw)
                ckv_s, kr_s = mla_latent(hs, pos_s, *kvw)
            j = i - N_A
            qw = (q_w_dq[j], q_g_cq[j], q_w_uq[j], q_g_qn[j], q_g_qr[j])
            qn_p, qr_p = mla_queries(rms_norm(hp, ln_mix[i]), pos_p, *qw)
            qn_s, qr_s = mla_queries(rms_norm(hs, ln_mix[i]), pos_s, *qw)
            yp = mla_attend_prompt(qn_p, qr_p, ckv_p, kr_p, kv_w_uk, kv_g_kn, kv_w_uv) @ a_w_o[j]
            ys = mla_attend_sample(qn_s, qr_s, ckv_s, kr_s, cache_ckv, cache_krope, page_table,
                                   kv_w_uk, kv_g_kn, kv_w_uv) @ a_w_o[j]
        hp = hp + yp
        hs = hs + ys
        pw = (peer_w_q[i], peer_g_q[i], peer_sub_keys[i], peer_u[i], peer_v[i])
        hp = hp + peer(rms_norm(hp, ln_ffn[i]), *pw)
        hs = hs + peer(rms_norm(hs, ln_ffn[i]), *pw)
        hp = hp + ple(hp, p_prompt[i], ln_ple[i], ple_w_proj[i], ple_w_gate[i])
        hs = hs + ple(hs, p_sample[i], ln_ple[i], ple_w_proj[i], ple_w_gate[i])
    return (hp, hs, jnp.stack(ssm_p), jnp.stack(conv_p), ckv_p, kr_p,
            jnp.stack(ssm_s), jnp.stack(conv_s), ckv_s, kr_s)
```

```python
import math

import jax
import jax.numpy as jnp
from jax import lax
from jax.experimental import pallas as pl
from jax.experimental.pallas import tpu as pltpu

D_MODEL = 1024
DEPTH = 2
PAGE_SIZE = 128
N_A = DEPTH // 2
D_INNER = 2 * D_MODEL
SSM_HEADDIM = 64
SSM_HEADS = D_INNER // SSM_HEADDIM
SSM_GROUPS = 4
SSM_STATE = 128
CONV_W = 4
CONV_DIM = D_INNER + 2 * SSM_GROUPS * SSM_STATE
SSD_CHUNK = 128
MLA_HEADS = 16
Q_LORA = 384
KV_LORA = 256
NOPE_DIM = 64
ROPE_DIM = 32
V_DIM = 64
ROPE_THETA = 10000.0
Q_BLOCK = 128
ATTN_SCALE = (NOPE_DIM + ROPE_DIM) ** -0.5
PEER_HEADS = 8
PEER_KEYS = 128
PEER_QDIM = 256
PEER_TOPK = 16
PEER_TOKEN_BLOCK = 512
PLE_DIM = 256
EPS = 1e-6


def rms_norm(x, g):
    xf = x.astype(jnp.float32)
    y = xf * lax.rsqrt(jnp.mean(xf * xf, -1, keepdims=True) + EPS)
    return (y * g.astype(jnp.float32)).astype(x.dtype)


def rope(x, pos):
    half = ROPE_DIM // 2
    inv = ROPE_THETA ** (-jnp.arange(half, dtype=jnp.float32) / half)
    ang = pos.astype(jnp.float32)[:, None] * inv
    ang = ang.reshape((1, ang.shape[0]) + (1,) * (x.ndim - 3) + (half,))
    cos, sin = jnp.cos(ang), jnp.sin(ang)
    xf = x.astype(jnp.float32)
    x1, x2 = xf[..., :half], xf[..., half:]
    return jnp.concatenate([x1 * cos - x2 * sin, x1 * sin + x2 * cos], -1).astype(x.dtype)


def causal_dwconv(u, prev, w, b):
    full = jnp.concatenate([prev, u], 1)
    y = lax.conv_general_dilated(full, w.astype(full.dtype)[:, None, :], (1,), 'VALID',
                                 dimension_numbers=('NWC', 'WIO', 'NWC'),
                                 feature_group_count=u.shape[-1])
    return y + b, full[:, -(CONV_W - 1):]


def segsum(a):
    cs = jnp.cumsum(a, -1)
    q = a.shape[-1]
    mask = jnp.tril(jnp.ones((q, q), bool))
    return jnp.where(mask, cs[..., :, None] - cs[..., None, :], -jnp.inf)


def ssd(x, dt, A, Bm, Cm, h0):
    b, l = x.shape[:2]
    q = SSD_CHUNK if l % SSD_CHUNK == 0 else l
    c = l // q
    G, R, P, N = SSM_GROUPS, SSM_HEADS // SSM_GROUPS, SSM_HEADDIM, SSM_STATE
    xd = (x * dt[..., None]).reshape(b, c, q, G, R, P)
    a = (dt * A).reshape(b, c, q, G, R).transpose(0, 3, 4, 1, 2)
    a_cum = jnp.cumsum(a, -1)
    Bc = Bm.reshape(b, c, q, G, N)
    Cc = Cm.reshape(b, c, q, G, N)
    Lmat = jnp.exp(segsum(a))
    cb = jnp.einsum('bclgn,bcsgn->bgcls', Cc, Bc)
    y_diag = jnp.einsum('bgcls,bgrcls,bcsgrp->bclgrp', cb, Lmat, xd)
    decay = jnp.exp(a_cum[..., -1:] - a_cum)
    st = jnp.einsum('bclgn,bgrcl,bclgrp->bcgrpn', Bc, decay, xd)
    chunk_decay = jnp.exp(a_cum[..., -1])

    def step(h, inp):
        s_c, d_c = inp
        return h * d_c[..., None, None] + s_c, h

    h_last, h_in = lax.scan(step, h0.reshape(b, G, R, P, N),
                            (st.transpose(1, 0, 2, 3, 4, 5), chunk_decay.transpose(3, 0, 1, 2)))
    y_off = jnp.einsum('bclgn,cbgrpn,bgrcl->bclgrp', Cc, h_in, jnp.exp(a_cum))
    y = (y_diag + y_off).reshape(b, l, SSM_HEADS, P)
    return y, h_last.reshape(b, SSM_HEADS, P, N)


def mamba_mixer(u, conv_prev, h0, w_in, conv_w, conv_b, dt_bias, a_log, d_skip, g_norm, w_out):
    b, l, _ = u.shape
    f32 = jnp.float32
    z, xbc, dt = jnp.split(u @ w_in, [D_INNER, D_INNER + CONV_DIM], -1)
    xbc, conv_new = causal_dwconv(xbc, conv_prev, conv_w, conv_b)
    xbc = jax.nn.silu(xbc)
    xs, Bm, Cm = jnp.split(xbc, [D_INNER, D_INNER + SSM_GROUPS * SSM_STATE], -1)
    dt = jax.nn.softplus(dt.astype(f32) + dt_bias.astype(f32))
    A = -jnp.exp(a_log.astype(f32))
    xs = xs.astype(f32).reshape(b, l, SSM_HEADS, SSM_HEADDIM)
    y, h_new = ssd(xs, dt, A,
                   Bm.astype(f32).reshape(b, l, SSM_GROUPS, SSM_STATE),
                   Cm.astype(f32).reshape(b, l, SSM_GROUPS, SSM_STATE),
                   h0.astype(f32))
    y = y + d_skip.astype(f32)[:, None] * xs
    y = y.reshape(b, l, D_INNER) * jax.nn.silu(z.astype(f32))
    y = rms_norm(y.reshape(b, l, SSM_GROUPS, D_INNER // SSM_GROUPS),
                 g_norm.reshape(SSM_GROUPS, -1)).reshape(b, l, D_INNER)
    return y.astype(u.dtype) @ w_out, conv_new, h_new.astype(u.dtype)


def mla_latent(h, pos, g_in, w_dkv, g_ckv, w_kr, g_kr):
    hn = rms_norm(h, g_in)
    ckv = rms_norm(hn @ w_dkv, g_ckv)
    kr = rope(rms_norm(hn @ w_kr, g_kr), pos)
    return ckv, kr


def mla_queries(u, pos, w_dq, g_cq, w_uq, g_qn, g_qr):
    b, l, _ = u.shape
    cq = rms_norm(u @ w_dq, g_cq)
    q = (cq @ w_uq).reshape(b, l, MLA_HEADS, NOPE_DIM + ROPE_DIM)
    qn = rms_norm(q[..., :NOPE_DIM], g_qn)
    qr = rope(rms_norm(q[..., NOPE_DIM:], g_qr), pos)
    return qn, qr


def expand_keys(ckv, w_uk, g_kn):
    return rms_norm((ckv @ w_uk).reshape(ckv.shape[:2] + (MLA_HEADS, NOPE_DIM)), g_kn)


def mla_attend_prompt(qn, qr, ckv, kr, w_uk, g_kn, w_uv):
    b, l = qn.shape[:2]
    kn = expand_keys(ckv, w_uk, g_kn)
    v = (ckv @ w_uv).reshape(b, l, MLA_HEADS, V_DIM)
    nblk = l // Q_BLOCK
    kpos = jnp.arange(l)

    def block(args):
        qn_b, qr_b, start = args
        s = jnp.einsum('bqhd,bkhd->bhqk', qn_b, kn, preferred_element_type=jnp.float32)
        s = s + jnp.einsum('bqhd,bkd->bhqk', qr_b, kr, preferred_element_type=jnp.float32)
        qpos = start + jnp.arange(Q_BLOCK)
        s = jnp.where(kpos[None, :] <= qpos[:, None], s * ATTN_SCALE, -jnp.inf)
        p = jax.nn.softmax(s, -1)
        return jnp.einsum('bhqk,bkhd->bqhd', p.astype(v.dtype), v)

    qn_blk = qn.reshape(b, nblk, Q_BLOCK, MLA_HEADS, NOPE_DIM).transpose(1, 0, 2, 3, 4)
    qr_blk = qr.reshape(b, nblk, Q_BLOCK, MLA_HEADS, ROPE_DIM).transpose(1, 0, 2, 3, 4)
    out = lax.map(block, (qn_blk, qr_blk, jnp.arange(nblk) * Q_BLOCK))
    return out.transpose(1, 0, 2, 3, 4).reshape(b, l, MLA_HEADS * V_DIM)


def mla_attend_sample(qn, qr, ckv_new, kr_new, cache_ckv, cache_krope, page_table, w_uk, g_kn, w_uv):
    b, lq = qn.shape[:2]
    f32 = jnp.float32

    def scores(c_blk, kr_blk):
        kn = expand_keys(c_blk, w_uk, g_kn)
        s = jnp.einsum('bqhd,bkhd->bhqk', qn, kn, preferred_element_type=f32)
        s = s + jnp.einsum('bqhd,bkd->bhqk', qr, kr_blk, preferred_element_type=f32)
        return s * ATTN_SCALE

    def update(carry, s, c_blk):
        m, lsum, acc = carry
        m_new = jnp.maximum(m, s.max(-1))
        alpha = jnp.exp(m - m_new)
        p = jnp.exp(s - m_new[..., None])
        acc = acc * alpha[..., None] + jnp.einsum('bhqk,bkc->bhqc', p, c_blk.astype(f32))
        return (m_new, lsum * alpha + p.sum(-1), acc)

    def page_step(carry, pages):
        c_blk = cache_ckv[pages]
        kr_blk = cache_krope[pages]
        return update(carry, scores(c_blk, kr_blk), c_blk), None

    init = (jnp.full((b, MLA_HEADS, lq), -jnp.inf, f32),
            jnp.zeros((b, MLA_HEADS, lq), f32),
            jnp.zeros((b, MLA_HEADS, lq, KV_LORA), f32))
    carry, _ = lax.scan(page_step, init, page_table.T)
    causal = jnp.tril(jnp.ones((lq, lq), bool))
    s_new = jnp.where(causal, scores(ckv_new, kr_new), -jnp.inf)
    _, lsum, acc = update(carry, s_new, ckv_new)
    o_lat = acc / lsum[..., None]
    o = jnp.einsum('bhqc,chd->bqhd', o_lat, w_uv.astype(f32).reshape(KV_LORA, MLA_HEADS, V_DIM))
    return o.reshape(b, lq, MLA_HEADS * V_DIM).astype(qn.dtype)


def peer(u, w_q, g_q, sub_keys, u_tab, v_tab):
    b, l, d = u.shape
    t = b * l
    nblk = -(-t // PEER_TOKEN_BLOCK)
    flat = jnp.pad(u.reshape(t, d), ((0, nblk * PEER_TOKEN_BLOCK - t), (0, 0)))
    half = PEER_QDIM // 2

    def block(xb):
        q = rms_norm((xb @ w_q).reshape(-1, PEER_HEADS, PEER_QDIM), g_q)
        s1 = jnp.einsum('thd,kd->thk', q[..., :half], sub_keys[0], preferred_element_type=jnp.float32)
        s2 = jnp.einsum('thd,kd->thk', q[..., half:], sub_keys[1], preferred_element_type=jnp.float32)
        v1, i1 = lax.top_k(s1, PEER_TOPK)
        v2, i2 = lax.top_k(s2, PEER_TOPK)
        cand = (v1[..., :, None] + v2[..., None, :]).reshape(xb.shape[0], PEER_HEADS, PEER_TOPK * PEER_TOPK)
        vals, ci = lax.top_k(cand, PEER_TOPK)
        e1 = jnp.take_along_axis(i1, ci // PEER_TOPK, -1)
        e2 = jnp.take_along_axis(i2, ci % PEER_TOPK, -1)
        eidx = e1 * PEER_KEYS + e2
        gates = jax.nn.softmax(vals, -1)
        act = jax.nn.gelu(jnp.einsum('td,thkd->thk', xb, u_tab[eidx]), approximate=False)
        return jnp.einsum('thk,thkd->td', (gates * act).astype(xb.dtype), v_tab[eidx])

    out = lax.map(block, flat.reshape(nblk, PEER_TOKEN_BLOCK, d))
    return out.reshape(-1, d)[:t].reshape(b, l, d)


def ple(h, p, g, w_proj, w_gate):
    return (p @ w_proj) * jax.nn.sigmoid(rms_norm(h, g) @ w_gate)


def _add_kernel(a_ref, b_ref, o_ref):
    o_ref[...] = a_ref[...] + b_ref[...]


def _pallas_add(a, b):
    shp = a.shape
    a2 = a.reshape(-1, shp[-1])
    b2 = b.reshape(-1, shp[-1])
    tr = 512
    spec = pl.BlockSpec((tr, shp[-1]), lambda i: (i, 0))
    out = pl.pallas_call(
        _add_kernel,
        out_shape=jax.ShapeDtypeStruct(a2.shape, a2.dtype),
        grid=(a2.shape[0] // tr,),
        in_specs=[spec, spec],
        out_specs=spec,
    )(a2, b2)
    return out.reshape(shp)


def kernel(x_prompt, x_sample, p_prompt, p_sample, state_ssm, state_conv, cache_ckv, cache_krope, page_table,
           ln_mix, ln_ffn, ln_ple,
           m_w_in, m_conv_w, m_conv_b, m_dt_bias, m_a_log, m_d, m_g_norm, m_w_out,
           kv_g_in, kv_w_dkv, kv_g_ckv, kv_w_kr, kv_g_kr, kv_w_uk, kv_g_kn, kv_w_uv,
           q_w_dq, q_g_cq, q_w_uq, q_g_qn, q_g_qr, a_w_o,
           peer_w_q, peer_g_q, peer_sub_keys, peer_u, peer_v,
           ple_w_proj, ple_w_gate):
    bp, lp = x_prompt.shape[:2]
    pos_p = jnp.arange(lp)
    pos_s = page_table.shape[1] * PAGE_SIZE + jnp.arange(x_sample.shape[1])
    hp, hs = x_prompt, x_sample
    ssm_p, conv_p, ssm_s, conv_s = [], [], [], []
    for i in range(DEPTH):
        if i < N_A:
            mw = (m_w_in[i], m_conv_w[i], m_conv_b[i], m_dt_bias[i], m_a_log[i], m_d[i], m_g_norm[i], m_w_out[i])
            conv0 = jnp.zeros((bp, CONV_W - 1, CONV_DIM), hp.dtype)
            ssm0 = jnp.zeros((bp, SSM_HEADS, SSM_HEADDIM, SSM_STATE), hp.dtype)
            yp, c_p, s_p = mamba_mixer(rms_norm(hp, ln_mix[i]), conv0, ssm0, *mw)
            ys, c_s, s_s = mamba_mixer(rms_norm(hs, ln_mix[i]), state_conv[i], state_ssm[i], *mw)
            conv_p.append(c_p)
            ssm_p.append(s_p)
            conv_s.append(c_s)
            ssm_s.append(s_s)
        else:
            if i == N_A:
                kvw = (kv_g_in, kv_w_dkv, kv_g_ckv, kv_w_kr, kv_g_kr)
                ckv_p, kr_p = mla_latent(hp, pos_p, *kvw)
                ckv_s, kr_s = mla_latent(hs, pos_s, *kvw)
            j = i - N_A
            qw = (q_w_dq[j], q_g_cq[j], q_w_uq[j], q_g_qn[j], q_g_qr[j])
            qn_p, qr_p = mla_queries(rms_norm(hp, ln_mix[i]), pos_p, *qw)
            qn_s, qr_s = mla_queries(rms_norm(hs, ln_mix[i]), pos_s, *qw)
            yp = mla_attend_prompt(qn_p, qr_p, ckv_p, kr_p, kv_w_uk, kv_g_kn, kv_w_uv) @ a_w_o[j]
            ys = mla_attend_sample(qn_s, qr_s, ckv_s, kr_s, cache_ckv, cache_krope, page_table,
                                   kv_w_uk, kv_g_kn, kv_w_uv) @ a_w_o[j]
        hp = hp + yp
        hs = hs + ys
        pw = (peer_w_q[i], peer_g_q[i], peer_sub_keys[i], peer_u[i], peer_v[i])
        hp = hp + peer(rms_norm(hp, ln_ffn[i]), *pw)
        hs = hs + peer(rms_norm(hs, ln_ffn[i]), *pw)
        hp = hp + ple(hp, p_prompt[i], ln_ple[i], ple_w_proj[i], ple_w_gate[i])
        if i == DEPTH - 1:
            hs = _pallas_add(hs, ple(hs, p_sample[i], ln_ple[i], ple_w_proj[i], ple_w_gate[i]))
        else:
            hs = hs + ple(hs, p_sample[i], ln_ple[i], ple_w_proj[i], ple_w_gate[i])
    return (hp, hs, jnp.stack(ssm_p), jnp.stack(conv_p), ckv_p, kr_p,
            jnp.stack(ssm_s), jnp.stack(conv_s), ckv_s, kr_s)
```

```python
import math

import jax
import jax.numpy as jnp
from jax import lax
from jax.experimental import pallas as pl
from jax.experimental.pallas import tpu as pltpu

D_MODEL = 1024
DEPTH = 2
PAGE_SIZE = 128
N_A = DEPTH // 2
D_INNER = 2 * D_MODEL
SSM_HEADDIM = 64
SSM_HEADS = D_INNER // SSM_HEADDIM
SSM_GROUPS = 4
SSM_STATE = 128
CONV_W = 4
CONV_DIM = D_INNER + 2 * SSM_GROUPS * SSM_STATE
SSD_CHUNK = 128
MLA_HEADS = 16
Q_LORA = 384
KV_LORA = 256
NOPE_DIM = 64
ROPE_DIM = 32
V_DIM = 64
ROPE_THETA = 10000.0
Q_BLOCK = 128
ATTN_SCALE = (NOPE_DIM + ROPE_DIM) ** -0.5
PEER_HEADS = 8
PEER_KEYS = 128
PEER_QDIM = 256
PEER_TOPK = 16
PEER_TOKEN_BLOCK = 512
PLE_DIM = 256
EPS = 1e-6


def rms_norm(x, g):
    xf = x.astype(jnp.float32)
    y = xf * lax.rsqrt(jnp.mean(xf * xf, -1, keepdims=True) + EPS)
    return (y * g.astype(jnp.float32)).astype(x.dtype)


def rope(x, pos):
    half = ROPE_DIM // 2
    inv = ROPE_THETA ** (-jnp.arange(half, dtype=jnp.float32) / half)
    ang = pos.astype(jnp.float32)[:, None] * inv
    ang = ang.reshape((1, ang.shape[0]) + (1,) * (x.ndim - 3) + (half,))
    cos, sin = jnp.cos(ang), jnp.sin(ang)
    xf = x.astype(jnp.float32)
    x1, x2 = xf[..., :half], xf[..., half:]
    return jnp.concatenate([x1 * cos - x2 * sin, x1 * sin + x2 * cos], -1).astype(x.dtype)


def causal_dwconv(u, prev, w, b):
    full = jnp.concatenate([prev, u], 1)
    y = lax.conv_general_dilated(full, w.astype(full.dtype)[:, None, :], (1,), 'VALID',
                                 dimension_numbers=('NWC', 'WIO', 'NWC'),
                                 feature_group_count=u.shape[-1])
    return y + b, full[:, -(CONV_W - 1):]


def segsum(a):
    cs = jnp.cumsum(a, -1)
    q = a.shape[-1]
    mask = jnp.tril(jnp.ones((q, q), bool))
    return jnp.where(mask, cs[..., :, None] - cs[..., None, :], -jnp.inf)


def ssd(x, dt, A, Bm, Cm, h0):
    b, l = x.shape[:2]
    q = SSD_CHUNK if l % SSD_CHUNK == 0 else l
    c = l // q
    G, R, P, N = SSM_GROUPS, SSM_HEADS // SSM_GROUPS, SSM_HEADDIM, SSM_STATE
    xd = (x * dt[..., None]).reshape(b, c, q, G, R, P)
    a = (dt * A).reshape(b, c, q, G, R).transpose(0, 3, 4, 1, 2)
    a_cum = jnp.cumsum(a, -1)
    Bc = Bm.reshape(b, c, q, G, N)
    Cc = Cm.reshape(b, c, q, G, N)
    Lmat = jnp.exp(segsum(a))
    cb = jnp.einsum('bclgn,bcsgn->bgcls', Cc, Bc)
    y_diag = jnp.einsum('bgcls,bgrcls,bcsgrp->bclgrp', cb, Lmat, xd)
    decay = jnp.exp(a_cum[..., -1:] - a_cum)
    st = jnp.einsum('bclgn,bgrcl,bclgrp->bcgrpn', Bc, decay, xd)
    chunk_decay = jnp.exp(a_cum[..., -1])

    def step(h, inp):
        s_c, d_c = inp
        return h * d_c[..., None, None] + s_c, h

    h_last, h_in = lax.scan(step, h0.reshape(b, G, R, P, N),
                            (st.transpose(1, 0, 2, 3, 4, 5), chunk_decay.transpose(3, 0, 1, 2)))
    y_off = jnp.einsum('bclgn,cbgrpn,bgrcl->bclgrp', Cc, h_in, jnp.exp(a_cum))
    y = (y_diag + y_off).reshape(b, l, SSM_HEADS, P)
    return y, h_last.reshape(b, SSM_HEADS, P, N)


def mamba_mixer(u, conv_prev, h0, w_in, conv_w, conv_b, dt_bias, a_log, d_skip, g_norm, w_out):
    b, l, _ = u.shape
    f32 = jnp.float32
    z, xbc, dt = jnp.split(u @ w_in, [D_INNER, D_INNER + CONV_DIM], -1)
    xbc, conv_new = causal_dwconv(xbc, conv_prev, conv_w, conv_b)
    xbc = jax.nn.silu(xbc)
    xs, Bm, Cm = jnp.split(xbc, [D_INNER, D_INNER + SSM_GROUPS * SSM_STATE], -1)
    dt = jax.nn.softplus(dt.astype(f32) + dt_bias.astype(f32))
    A = -jnp.exp(a_log.astype(f32))
    xs = xs.astype(f32).reshape(b, l, SSM_HEADS, SSM_HEADDIM)
    y, h_new = ssd(xs, dt, A,
                   Bm.astype(f32).reshape(b, l, SSM_GROUPS, SSM_STATE),
                   Cm.astype(f32).reshape(b, l, SSM_GROUPS, SSM_STATE),
                   h0.astype(f32))
    y = y + d_skip.astype(f32)[:, None] * xs
    y = y.reshape(b, l, D_INNER) * jax.nn.silu(z.astype(f32))
    y = rms_norm(y.reshape(b, l, SSM_GROUPS, D_INNER // SSM_GROUPS),
                 g_norm.reshape(SSM_GROUPS, -1)).reshape(b, l, D_INNER)
    return y.astype(u.dtype) @ w_out, conv_new, h_new.astype(u.dtype)


def mla_latent(h, pos, g_in, w_dkv, g_ckv, w_kr, g_kr):
    hn = rms_norm(h, g_in)
    ckv = rms_norm(hn @ w_dkv, g_ckv)
    kr = rope(rms_norm(hn @ w_kr, g_kr), pos)
    return ckv, kr


def mla_queries(u, pos, w_dq, g_cq, w_uq, g_qn, g_qr):
    b, l, _ = u.shape
    cq = rms_norm(u @ w_dq, g_cq)
    q = (cq @ w_uq).reshape(b, l, MLA_HEADS, NOPE_DIM + ROPE_DIM)
    qn = rms_norm(q[..., :NOPE_DIM], g_qn)
    qr = rope(rms_norm(q[..., NOPE_DIM:], g_qr), pos)
    return qn, qr


def expand_keys(ckv, w_uk, g_kn):
    return rms_norm((ckv @ w_uk).reshape(ckv.shape[:2] + (MLA_HEADS, NOPE_DIM)), g_kn)


def mla_attend_prompt(qn, qr, ckv, kr, w_uk, g_kn, w_uv):
    b, l = qn.shape[:2]
    kn = expand_keys(ckv, w_uk, g_kn)
    v = (ckv @ w_uv).reshape(b, l, MLA_HEADS, V_DIM)
    nblk = l // Q_BLOCK
    kpos = jnp.arange(l)

    def block(args):
        qn_b, qr_b, start = args
        s = jnp.einsum('bqhd,bkhd->bhqk', qn_b, kn, preferred_element_type=jnp.float32)
        s = s + jnp.einsum('bqhd,bkd->bhqk', qr_b, kr, preferred_element_type=jnp.float32)
        qpos = start + jnp.arange(Q_BLOCK)
        s = jnp.where(kpos[None, :] <= qpos[:, None], s * ATTN_SCALE, -jnp.inf)
        p = jax.nn.softmax(s, -1)
        return jnp.einsum('bhqk,bkhd->bqhd', p.astype(v.dtype), v)

    qn_blk = qn.reshape(b, nblk, Q_BLOCK, MLA_HEADS, NOPE_DIM).transpose(1, 0, 2, 3, 4)
    qr_blk = qr.reshape(b, nblk, Q_BLOCK, MLA_HEADS, ROPE_DIM).transpose(1, 0, 2, 3, 4)
    out = lax.map(block, (qn_blk, qr_blk, jnp.arange(nblk) * Q_BLOCK))
    return out.transpose(1, 0, 2, 3, 4).reshape(b, l, MLA_HEADS * V_DIM)


def mla_attend_sample(qn, qr, ckv_new, kr_new, cache_ckv, cache_krope, page_table, w_uk, g_kn, w_uv):
    b, lq = qn.shape[:2]
    f32 = jnp.float32

    def scores(c_blk, kr_blk):
        kn = expand_keys(c_blk, w_uk, g_kn)
        s = jnp.einsum('bqhd,bkhd->bhqk', qn, kn, preferred_element_type=f32)
        s = s + jnp.einsum('bqhd,bkd->bhqk', qr, kr_blk, preferred_element_type=f32)
        return s * ATTN_SCALE

    def update(carry, s, c_blk):
        m, lsum, acc = carry
        m_new = jnp.maximum(m, s.max(-1))
        alpha = jnp.exp(m - m_new)
        p = jnp.exp(s - m_new[..., None])
        acc = acc * alpha[..., None] + jnp.einsum('bhqk,bkc->bhqc', p, c_blk.astype(f32))
        return (m_new, lsum * alpha + p.sum(-1), acc)

    def page_step(carry, pages):
        c_blk = cache_ckv[pages]
        kr_blk = cache_krope[pages]
        return update(carry, scores(c_blk, kr_blk), c_blk), None

    init = (jnp.full((b, MLA_HEADS, lq), -jnp.inf, f32),
            jnp.zeros((b, MLA_HEADS, lq), f32),
            jnp.zeros((b, MLA_HEADS, lq, KV_LORA), f32))
    carry, _ = lax.scan(page_step, init, page_table.T)
    causal = jnp.tril(jnp.ones((lq, lq), bool))
    s_new = jnp.where(causal, scores(ckv_new, kr_new), -jnp.inf)
    _, lsum, acc = update(carry, s_new, ckv_new)
    o_lat = acc / lsum[..., None]
    o = jnp.einsum('bhqc,chd->bqhd', o_lat, w_uv.astype(f32).reshape(KV_LORA, MLA_HEADS, V_DIM))
    return o.reshape(b, lq, MLA_HEADS * V_DIM).astype(qn.dtype)


PEER_ROUTE_TOKENS = 256
PEER_LANE_TILE = 128
PEER_GATHER_TOKENS = 8
PEER_ROW_TILE = (8, 128)
PEER_PAIRS = PEER_HEADS * PEER_TOPK


def _topk_rows(s, k):
    n = s.shape[0]
    iota = lax.broadcasted_iota(jnp.int32, s.shape, 0).astype(jnp.float32)
    vals, idxs = [], []
    for _ in range(k):
        m = jnp.max(s, axis=0, keepdims=True)
        ix = jnp.min(jnp.where(s == m, iota, float(n)), axis=0, keepdims=True)
        vals.append(m)
        idxs.append(ix)
        s = jnp.where(iota == ix, -jnp.inf, s)
    return jnp.concatenate(vals, 0), jnp.concatenate(idxs, 0)


def _select_rows(table, sel):
    out = jnp.zeros(sel.shape, table.dtype)
    for j in range(table.shape[0]):
        out = jnp.where(sel == float(j), table[j:j + 1, :], out)
    return out


def _peer_route_kernel(x_ref, g_ref, wq_ref, gq_ref, k1_ref, k2_ref, xn_ref, idx_ref, gate_ref, xn_scr):
    f32, bf16 = jnp.float32, jnp.bfloat16

    @pl.when(pl.program_id(1) == 0)
    def _():
        x = x_ref[...]
        y = x * lax.rsqrt(jnp.mean(x * x, -1, keepdims=True) + EPS) * g_ref[...]
        xn_ref[...] = y
        xn_scr[...] = y.astype(bf16)

    q = jnp.dot(xn_scr[...], wq_ref[...], preferred_element_type=f32)
    qn = (q * lax.rsqrt(jnp.mean(q * q, -1, keepdims=True) + EPS) * gq_ref[...]).astype(bf16)
    half = PEER_QDIM // 2
    nt = (((1,), (1,)), ((), ()))
    s1 = lax.dot_general(k1_ref[...], qn[:, :half], nt, preferred_element_type=f32)
    s2 = lax.dot_general(k2_ref[...], qn[:, half:], nt, preferred_element_type=f32)
    for c in range(s1.shape[1] // PEER_LANE_TILE):
        cs = slice(c * PEER_LANE_TILE, (c + 1) * PEER_LANE_TILE)
        v1, i1 = _topk_rows(s1[:, cs], PEER_TOPK)
        v2, i2 = _topk_rows(s2[:, cs], PEER_TOPK)
        cand = jnp.concatenate([v1[a:a + 1, :] + v2 for a in range(PEER_TOPK)], 0)
        vals, ci = _topk_rows(cand, PEER_TOPK)
        a = jnp.floor(ci * (1.0 / PEER_TOPK))
        b = ci - a * PEER_TOPK
        e = _select_rows(i1, a) * float(PEER_KEYS) + _select_rows(i2, b)
        ex = jnp.exp(vals - vals[0:1, :])
        idx_ref[:, cs] = e.astype(jnp.int32)
        gate_ref[:, cs] = ex / jnp.sum(ex, axis=0, keepdims=True)


def _peer_route(x, g, w_q, g_q, sub_keys):
    t, d = x.shape
    tb = PEER_ROUTE_TOKENS
    assert t % tb == 0
    bf16 = jnp.bfloat16
    half = PEER_QDIM // 2
    return pl.pallas_call(
        _peer_route_kernel,
        out_shape=(jax.ShapeDtypeStruct((t, d), jnp.float32),
                   jax.ShapeDtypeStruct((PEER_PAIRS, t), jnp.int32),
                   jax.ShapeDtypeStruct((PEER_PAIRS, t), jnp.float32)),
        grid=(t // tb, PEER_HEADS),
        in_specs=[pl.BlockSpec((tb, d), lambda i, h: (i, 0)),
                  pl.BlockSpec((1, d), lambda i, h: (0, 0)),
                  pl.BlockSpec((d, PEER_QDIM), lambda i, h: (0, h)),
                  pl.BlockSpec((1, PEER_QDIM), lambda i, h: (0, 0)),
                  pl.BlockSpec((PEER_KEYS, half), lambda i, h: (0, 0)),
                  pl.BlockSpec((PEER_KEYS, half), lambda i, h: (0, 0))],
        out_specs=(pl.BlockSpec((tb, d), lambda i, h: (i, 0)),
                   pl.BlockSpec((PEER_TOPK, tb), lambda i, h: (h, i)),
                   pl.BlockSpec((PEER_TOPK, tb), lambda i, h: (h, i))),
        scratch_shapes=[pltpu.VMEM((tb, d), bf16)],
        compiler_params=pltpu.CompilerParams(dimension_semantics=("arbitrary", "arbitrary")),
        name="peer_route",
    )(x, g.reshape(1, d), w_q.astype(bf16), g_q.reshape(1, PEER_QDIM),
      sub_keys[0].astype(bf16), sub_keys[1].astype(bf16))


def _peer_expert_kernel(idx_ref, idx_next_ref, x_ref, gate_ref, diag_ref, fold_ref, spread_ref, u_hbm, v_hbm,
                        out_ref, ubuf, vbuf, sems):
    f32, bf16 = jnp.float32, jnp.bfloat16
    tb = PEER_GATHER_TOKENS
    rows = tb * PEER_PAIRS
    i = pl.program_id(0)
    n = pl.num_programs(0)
    slot = i % 2

    def start_gather(ids_ref, dst_slot):
        def per_token(t, carry):
            base = dst_slot * rows + t * PEER_PAIRS
            for k in range(PEER_PAIRS):
                e = ids_ref[t, k]
                pltpu.make_async_copy(u_hbm.at[e], ubuf.at[base + k], sems.at[0, dst_slot]).start()
                pltpu.make_async_copy(v_hbm.at[e], vbuf.at[base + k], sems.at[1, dst_slot]).start()
            return carry
        lax.fori_loop(0, tb, per_token, 0)

    @pl.when(i == 0)
    def _():
        start_gather(idx_ref, 0)

    @pl.when(i + 1 < n)
    def _():
        start_gather(idx_next_ref, 1 - slot)

    pltpu.make_async_copy(u_hbm.at[pl.ds(0, rows)], ubuf.at[pl.ds(slot * rows, rows)], sems.at[0, slot]).wait()
    pltpu.make_async_copy(v_hbm.at[pl.ds(0, rows)], vbuf.at[pl.ds(slot * rows, rows)], sems.at[1, slot]).wait()

    nt = (((1,), (1,)), ((), ()))
    diag = diag_ref[...]
    flat = (PEER_PAIRS * PEER_ROW_TILE[0], PEER_ROW_TILE[1])
    for t in range(tb):
        base = slot * rows + t * PEER_PAIRS
        ub = ubuf[pl.ds(base, PEER_PAIRS)].reshape(flat).astype(bf16)
        g = lax.dot_general(x_ref[t].astype(bf16), ub, nt, preferred_element_type=f32)
        e = g * diag
        e_hi = e.astype(bf16)
        e_lo = (e - e_hi.astype(f32)).astype(bf16)
        h = jnp.dot(jnp.concatenate([e_hi, e_lo], 0), fold_ref[...], preferred_element_type=f32)
        h = jnp.sum(h, axis=0, keepdims=True)
        act = 0.5 * h * (1.0 + lax.erf(h * math.sqrt(0.5)))
        w = (gate_ref[t:t + 1, :] * act).astype(bf16)
        wfull = jnp.dot(jnp.broadcast_to(w, (PEER_ROW_TILE[0], PEER_PAIRS)), spread_ref[...],
                        preferred_element_type=f32)
        wm = (wfull * diag).astype(bf16)
        vb = vbuf[pl.ds(base, PEER_PAIRS)].reshape(flat).astype(bf16)
        out_ref[t] = jnp.dot(wm, vb, preferred_element_type=f32)


def _peer_experts(xn, idx, gates, u_tab, v_tab):
    t, d = xn.shape
    tb = PEER_GATHER_TOKENS
    assert t % tb == 0 and d == PEER_ROW_TILE[0] * PEER_ROW_TILE[1]
    nblk = t // tb
    rows = tb * PEER_PAIRS
    n_exp = u_tab.shape[0]
    lane = jnp.arange(d)
    diag = (lane[None, :] % PEER_ROW_TILE[0] == jnp.arange(PEER_ROW_TILE[0])[:, None]).astype(jnp.float32)
    fold = (lane[:, None] // PEER_ROW_TILE[0] == jnp.arange(PEER_PAIRS)[None, :]).astype(jnp.bfloat16)
    out = pl.pallas_call(
        _peer_expert_kernel,
        out_shape=jax.ShapeDtypeStruct((t,) + PEER_ROW_TILE, jnp.float32),
        grid=(nblk,),
        in_specs=[pl.BlockSpec((tb, PEER_PAIRS), lambda i: (i, 0), memory_space=pltpu.SMEM),
                  pl.BlockSpec((tb, PEER_PAIRS), lambda i: (jnp.minimum(i + 1, nblk - 1), 0),
                               memory_space=pltpu.SMEM),
                  pl.BlockSpec((tb,) + PEER_ROW_TILE, lambda i: (i, 0, 0)),
                  pl.BlockSpec((tb, PEER_PAIRS), lambda i: (i, 0)),
                  pl.BlockSpec((PEER_ROW_TILE[0], d), lambda i: (0, 0)),
                  pl.BlockSpec((d, PEER_PAIRS), lambda i: (0, 0)),
                  pl.BlockSpec((PEER_PAIRS, d), lambda i: (0, 0)),
                  pl.BlockSpec(memory_space=pl.ANY),
                  pl.BlockSpec(memory_space=pl.ANY)],
        out_specs=pl.BlockSpec((tb,) + PEER_ROW_TILE, lambda i: (i, 0, 0)),
        scratch_shapes=[pltpu.VMEM((2 * rows,) + PEER_ROW_TILE, jnp.float32),
                        pltpu.VMEM((2 * rows,) + PEER_ROW_TILE, jnp.float32),
                        pltpu.SemaphoreType.DMA((2, 2))],
        compiler_params=pltpu.CompilerParams(dimension_semantics=("arbitrary",),
                                             vmem_limit_bytes=40 * 1024 * 1024),
        name="peer_experts",
    )(idx, idx, xn.reshape((t,) + PEER_ROW_TILE), gates, diag, fold, fold.T,
      u_tab.reshape((n_exp,) + PEER_ROW_TILE), v_tab.reshape((n_exp,) + PEER_ROW_TILE))
    return out.reshape(t, d)


def peer(x, g, w_q, g_q, sub_keys, u_tab, v_tab):
    xn, idx_t, gates_t = _peer_route(x, g, w_q, g_q, sub_keys)
    return _peer_experts(xn, idx_t.T, gates_t.T, u_tab, v_tab)


def ple(h, p, g, w_proj, w_gate):
    return (p @ w_proj) * jax.nn.sigmoid(rms_norm(h, g) @ w_gate)


def kernel(x_prompt, x_sample, p_prompt, p_sample, state_ssm, state_conv, cache_ckv, cache_krope, page_table,
           ln_mix, ln_ffn, ln_ple,
           m_w_in, m_conv_w, m_conv_b, m_dt_bias, m_a_log, m_d, m_g_norm, m_w_out,
           kv_g_in, kv_w_dkv, kv_g_ckv, kv_w_kr, kv_g_kr, kv_w_uk, kv_g_kn, kv_w_uv,
           q_w_dq, q_g_cq, q_w_uq, q_g_qn, q_g_qr, a_w_o,
           peer_w_q, peer_g_q, peer_sub_keys, peer_u, peer_v,
           ple_w_proj, ple_w_gate):
    bp, lp = x_prompt.shape[:2]
    pos_p = jnp.arange(lp)
    pos_s = page_table.shape[1] * PAGE_SIZE + jnp.arange(x_sample.shape[1])
    hp, hs = x_prompt, x_sample
    ssm_p, conv_p, ssm_s, conv_s = [], [], [], []
    for i in range(DEPTH):
        if i < N_A:
            mw = (m_w_in[i], m_conv_w[i], m_conv_b[i], m_dt_bias[i], m_a_log[i], m_d[i], m_g_norm[i], m_w_out[i])
            conv0 = jnp.zeros((bp, CONV_W - 1, CONV_DIM), hp.dtype)
            ssm0 = jnp.zeros((bp, SSM_HEADS, SSM_HEADDIM, SSM_STATE), hp.dtype)
            yp, c_p, s_p = mamba_mixer(rms_norm(hp, ln_mix[i]), conv0, ssm0, *mw)
            ys, c_s, s_s = mamba_mixer(rms_norm(hs, ln_mix[i]), state_conv[i], state_ssm[i], *mw)
            conv_p.append(c_p)
            ssm_p.append(s_p)
            conv_s.append(c_s)
            ssm_s.append(s_s)
        else:
            if i == N_A:
                kvw = (kv_g_in, kv_w_dkv, kv_g_ckv, kv_w_kr, kv_g_kr)
                ckv_p, kr_p = mla_latent(hp, pos_p, *kvw)
                ckv_s, kr_s = mla_latent(hs, pos_s, *kvw)
            j = i - N_A
            qw = (q_w_dq[j], q_g_cq[j], q_w_uq[j], q_g_qn[j], q_g_qr[j])
            qn_p, qr_p = mla_queries(rms_norm(hp, ln_mix[i]), pos_p, *qw)
            qn_s, qr_s = mla_queries(rms_norm(hs, ln_mix[i]), pos_s, *qw)
            yp = mla_attend_prompt(qn_p, qr_p, ckv_p, kr_p, kv_w_uk, kv_g_kn, kv_w_uv) @ a_w_o[j]
            ys = mla_attend_sample(qn_s, qr_s, ckv_s, kr_s, cache_ckv, cache_krope, page_table,
                                   kv_w_uk, kv_g_kn, kv_w_uv) @ a_w_o[j]
        n_p = bp * lp
        h_all = jnp.concatenate([(hp + yp).reshape(n_p, D_MODEL), (hs + ys).reshape(-1, D_MODEL)], 0)
        h_all = h_all + peer(h_all, ln_ffn[i], peer_w_q[i], peer_g_q[i], peer_sub_keys[i], peer_u[i], peer_v[i])
        hp = h_all[:n_p].reshape(hp.shape)
        hs = h_all[n_p:].reshape(hs.shape)
        hp = hp + ple(hp, p_prompt[i], ln_ple[i], ple_w_proj[i], ple_w_gate[i])
        hs = hs + ple(hs, p_sample[i], ln_ple[i], ple_w_proj[i], ple_w_gate[i])
    return (hp, hs, jnp.stack(ssm_p), jnp.stack(conv_p), ckv_p, kr_p,
            jnp.stack(ssm_s), jnp.stack(conv_s), ckv_s, kr_s)
```

```python
import math

import jax
import jax.numpy as jnp
from jax import lax
from jax.experimental import pallas as pl
from jax.experimental.pallas import tpu as pltpu

D_MODEL = 1024
DEPTH = 2
PAGE_SIZE = 128
N_A = DEPTH // 2
D_INNER = 2 * D_MODEL
SSM_HEADDIM = 64
SSM_HEADS = D_INNER // SSM_HEADDIM
SSM_GROUPS = 4
SSM_STATE = 128
CONV_W = 4
CONV_DIM = D_INNER + 2 * SSM_GROUPS * SSM_STATE
SSD_CHUNK = 128
MLA_HEADS = 16
Q_LORA = 384
KV_LORA = 256
NOPE_DIM = 64
ROPE_DIM = 32
V_DIM = 64
ROPE_THETA = 10000.0
Q_BLOCK = 128
ATTN_SCALE = (NOPE_DIM + ROPE_DIM) ** -0.5
PEER_HEADS = 8
PEER_KEYS = 128
PEER_QDIM = 256
PEER_TOPK = 16
PEER_TOKEN_BLOCK = 512
PLE_DIM = 256
EPS = 1e-6


def rms_norm(x, g):
    xf = x.astype(jnp.float32)
    y = xf * lax.rsqrt(jnp.mean(xf * xf, -1, keepdims=True) + EPS)
    return (y * g.astype(jnp.float32)).astype(x.dtype)


def rope(x, pos):
    half = ROPE_DIM // 2
    inv = ROPE_THETA ** (-jnp.arange(half, dtype=jnp.float32) / half)
    ang = pos.astype(jnp.float32)[:, None] * inv
    ang = ang.reshape((1, ang.shape[0]) + (1,) * (x.ndim - 3) + (half,))
    cos, sin = jnp.cos(ang), jnp.sin(ang)
    xf = x.astype(jnp.float32)
    x1, x2 = xf[..., :half], xf[..., half:]
    return jnp.concatenate([x1 * cos - x2 * sin, x1 * sin + x2 * cos], -1).astype(x.dtype)


def causal_dwconv(u, prev, w, b):
    full = jnp.concatenate([prev, u], 1)
    y = lax.conv_general_dilated(full, w.astype(full.dtype)[:, None, :], (1,), 'VALID',
                                 dimension_numbers=('NWC', 'WIO', 'NWC'),
                                 feature_group_count=u.shape[-1])
    return y + b, full[:, -(CONV_W - 1):]


def segsum(a):
    cs = jnp.cumsum(a, -1)
    q = a.shape[-1]
    mask = jnp.tril(jnp.ones((q, q), bool))
    return jnp.where(mask, cs[..., :, None] - cs[..., None, :], -jnp.inf)


def ssd(x, dt, A, Bm, Cm, h0):
    b, l = x.shape[:2]
    q = SSD_CHUNK if l % SSD_CHUNK == 0 else l
    c = l // q
    G, R, P, N = SSM_GROUPS, SSM_HEADS // SSM_GROUPS, SSM_HEADDIM, SSM_STATE
    xd = (x * dt[..., None]).reshape(b, c, q, G, R, P)
    a = (dt * A).reshape(b, c, q, G, R).transpose(0, 3, 4, 1, 2)
    a_cum = jnp.cumsum(a, -1)
    Bc = Bm.reshape(b, c, q, G, N)
    Cc = Cm.reshape(b, c, q, G, N)
    Lmat = jnp.exp(segsum(a))
    cb = jnp.einsum('bclgn,bcsgn->bgcls', Cc, Bc)
    y_diag = jnp.einsum('bgcls,bgrcls,bcsgrp->bclgrp', cb, Lmat, xd)
    decay = jnp.exp(a_cum[..., -1:] - a_cum)
    st = jnp.einsum('bclgn,bgrcl,bclgrp->bcgrpn', Bc, decay, xd)
    chunk_decay = jnp.exp(a_cum[..., -1])

    def step(h, inp):
        s_c, d_c = inp
        return h * d_c[..., None, None] + s_c, h

    h_last, h_in = lax.scan(step, h0.reshape(b, G, R, P, N),
                            (st.transpose(1, 0, 2, 3, 4, 5), chunk_decay.transpose(3, 0, 1, 2)))
    y_off = jnp.einsum('bclgn,cbgrpn,bgrcl->bclgrp', Cc, h_in, jnp.exp(a_cum))
    y = (y_diag + y_off).reshape(b, l, SSM_HEADS, P)
    return y, h_last.reshape(b, SSM_HEADS, P, N)


def mamba_mixer(u, conv_prev, h0, w_in, conv_w, conv_b, dt_bias, a_log, d_skip, g_norm, w_out):
    b, l, _ = u.shape
    f32 = jnp.float32
    z, xbc, dt = jnp.split(u @ w_in, [D_INNER, D_INNER + CONV_DIM], -1)
    xbc, conv_new = causal_dwconv(xbc, conv_prev, conv_w, conv_b)
    xbc = jax.nn.silu(xbc)
    xs, Bm, Cm = jnp.split(xbc, [D_INNER, D_INNER + SSM_GROUPS * SSM_STATE], -1)
    dt = jax.nn.softplus(dt.astype(f32) + dt_bias.astype(f32))
    A = -jnp.exp(a_log.astype(f32))
    xs = xs.astype(f32).reshape(b, l, SSM_HEADS, SSM_HEADDIM)
    y, h_new = ssd(xs, dt, A,
                   Bm.astype(f32).reshape(b, l, SSM_GROUPS, SSM_STATE),
                   Cm.astype(f32).reshape(b, l, SSM_GROUPS, SSM_STATE),
                   h0.astype(f32))
    y = y + d_skip.astype(f32)[:, None] * xs
    y = y.reshape(b, l, D_INNER) * jax.nn.silu(z.astype(f32))
    y = rms_norm(y.reshape(b, l, SSM_GROUPS, D_INNER // SSM_GROUPS),
                 g_norm.reshape(SSM_GROUPS, -1)).reshape(b, l, D_INNER)
    return y.astype(u.dtype) @ w_out, conv_new, h_new.astype(u.dtype)


def mla_latent(h, pos, g_in, w_dkv, g_ckv, w_kr, g_kr):
    hn = rms_norm(h, g_in)
    ckv = rms_norm(hn @ w_dkv, g_ckv)
    kr = rope(rms_norm(hn @ w_kr, g_kr), pos)
    return ckv, kr


def mla_queries(u, pos, w_dq, g_cq, w_uq, g_qn, g_qr):
    b, l, _ = u.shape
    cq = rms_norm(u @ w_dq, g_cq)
    q = (cq @ w_uq).reshape(b, l, MLA_HEADS, NOPE_DIM + ROPE_DIM)
    qn = rms_norm(q[..., :NOPE_DIM], g_qn)
    qr = rope(rms_norm(q[..., NOPE_DIM:], g_qr), pos)
    return qn, qr


def expand_keys(ckv, w_uk, g_kn):
    return rms_norm((ckv @ w_uk).reshape(ckv.shape[:2] + (MLA_HEADS, NOPE_DIM)), g_kn)


FLASH_TILE = 1024
FLASH_LANES = 128


def _flash_kernel(q_ref, k_ref, v_ref, o_ref, m_scr, l_scr, acc_scr):
    f32 = jnp.float32
    qi, ki = pl.program_id(2), pl.program_id(3)

    @pl.when(ki == 0)
    def _():
        m_scr[...] = jnp.full(m_scr.shape, -jnp.inf, f32)
        l_scr[...] = jnp.zeros(l_scr.shape, f32)
        acc_scr[...] = jnp.zeros(acc_scr.shape, f32)

    def update(on_diagonal):
        s = lax.dot_general(q_ref[...], k_ref[...], (((1,), (1,)), ((), ())), preferred_element_type=f32)
        s = s * ATTN_SCALE
        if on_diagonal:
            row = lax.broadcasted_iota(jnp.int32, s.shape, 0)
            col = lax.broadcasted_iota(jnp.int32, s.shape, 1)
            s = jnp.where(col <= row, s, -jnp.inf)
        m_prev = m_scr[...]
        m_new = jnp.maximum(m_prev, jnp.max(s, axis=-1, keepdims=True))
        alpha = jnp.exp(m_prev - m_new)
        p = jnp.exp(s - m_new)
        l_scr[...] = alpha * l_scr[...] + jnp.sum(p, axis=-1, keepdims=True)
        acc_scr[...] = alpha * acc_scr[...] + jnp.dot(p.astype(v_ref.dtype), v_ref[...],
                                                      preferred_element_type=f32)
        m_scr[...] = m_new

    @pl.when(ki < qi)
    def _():
        update(False)

    @pl.when(ki == qi)
    def _():
        update(True)
        o_ref[...] = acc_scr[...] / l_scr[...]


def _flash_attention(q, k, v):
    b, h, l, _ = q.shape
    t = min(FLASH_TILE, l)
    assert l % t == 0
    nblk = l // t
    return pl.pallas_call(
        _flash_kernel,
        out_shape=jax.ShapeDtypeStruct((b, h, l, V_DIM), jnp.float32),
        grid=(b, h, nblk, nblk),
        in_specs=[pl.BlockSpec((None, None, t, FLASH_LANES), lambda bi, hi, qi, ki: (bi, hi, qi, 0)),
                  pl.BlockSpec((None, None, t, FLASH_LANES),
                               lambda bi, hi, qi, ki: (bi, hi, jnp.minimum(ki, qi), 0)),
                  pl.BlockSpec((None, None, t, V_DIM), lambda bi, hi, qi, ki: (bi, hi, jnp.minimum(ki, qi), 0))],
        out_specs=pl.BlockSpec((None, None, t, V_DIM), lambda bi, hi, qi, ki: (bi, hi, qi, 0)),
        scratch_shapes=[pltpu.VMEM((t, 1), jnp.float32), pltpu.VMEM((t, 1), jnp.float32),
                        pltpu.VMEM((t, V_DIM), jnp.float32)],
        compiler_params=pltpu.CompilerParams(
            dimension_semantics=("arbitrary", "arbitrary", "arbitrary", "arbitrary"),
            vmem_limit_bytes=48 * 1024 * 1024),
        name="prompt_attention",
    )(q, k, v)


def mla_attend_prompt(qn, qr, ckv, kr, w_uk, g_kn, w_uv):
    b, l = qn.shape[:2]
    bf16 = jnp.bfloat16
    kn = expand_keys(ckv, w_uk, g_kn)
    v = (ckv @ w_uv).reshape(b, l, MLA_HEADS, V_DIM)
    pad = jnp.zeros((b, l, MLA_HEADS, FLASH_LANES - NOPE_DIM - ROPE_DIM), bf16)
    q_cat = jnp.concatenate([qn.astype(bf16), qr.astype(bf16), pad], -1).transpose(0, 2, 1, 3)
    kr_h = jnp.broadcast_to(kr.astype(bf16)[:, :, None, :], (b, l, MLA_HEADS, ROPE_DIM))
    k_cat = jnp.concatenate([kn.astype(bf16), kr_h, pad], -1).transpose(0, 2, 1, 3)
    return _flash_attention(q_cat, k_cat, v.astype(bf16).transpose(0, 2, 1, 3))


DECODE_PAGES = 4
DECODE_NEW_ROWS = 8


def _decode_kernel(pt_ref, *refs):
    f32, bf16 = jnp.float32, jnp.bfloat16
    npg = DECODE_PAGES
    c_refs, kr_refs = refs[:npg], refs[npg:2 * npg]
    (qn_ref, qr_ref, cnew_ref, krnew_ref, wuk_ref, ind_ref, o_ref, m_scr, l_scr, acc_scr) = refs[2 * npg:]
    j = pl.program_id(1)
    nt = (((1,), (1,)), ((), ()))
    rows = qn_ref.shape[0]

    @pl.when(j == 0)
    def _():
        m_scr[...] = jnp.full(m_scr.shape, -jnp.inf, f32)
        l_scr[...] = jnp.zeros(l_scr.shape, f32)
        acc_scr[...] = jnp.zeros(acc_scr.shape, f32)

    def scores(c, kr):
        cb = c.astype(bf16)
        k = jnp.dot(cb, wuk_ref[...], preferred_element_type=f32)
        ms = lax.dot_general(ind_ref[...], (k * k).astype(bf16), nt, preferred_element_type=f32)
        r = lax.rsqrt(ms[:rows] * (1.0 / NOPE_DIM) + EPS)
        s = r * lax.dot_general(qn_ref[...], k.astype(bf16), nt, preferred_element_type=f32)
        s = s + lax.dot_general(qr_ref[...], kr.astype(bf16), nt, preferred_element_type=f32)
        return s * ATTN_SCALE, cb

    def update(s, cb):
        m_prev = m_scr[...]
        m_new = jnp.maximum(m_prev, jnp.max(s, axis=-1, keepdims=True))
        alpha = jnp.exp(m_prev - m_new)
        p = jnp.exp(s - m_new)
        l_scr[...] = alpha * l_scr[...] + jnp.sum(p, axis=-1, keepdims=True)
        acc_scr[...] = alpha * acc_scr[...] + jnp.dot(p.astype(bf16), cb, preferred_element_type=f32)
        m_scr[...] = m_new

    c_all = jnp.concatenate([r[...] for r in c_refs], 0)
    kr_all = jnp.concatenate([r[...] for r in kr_refs], 0)
    update(*scores(c_all, kr_all))

    @pl.when(j == pl.num_programs(1) - 1)
    def _():
        s, cb = scores(cnew_ref[...], krnew_ref[...])
        key = lax.broadcasted_iota(jnp.int32, s.shape, 1)
        query = lax.broadcasted_iota(jnp.int32, s.shape, 0) // MLA_HEADS
        update(jnp.where(key <= query, s, -jnp.inf), cb)
        o_ref[...] = acc_scr[...] / l_scr[...]


def mla_attend_sample(qn, qr, ckv_new, kr_new, cache_ckv, cache_krope, page_table, w_uk, g_kn, w_uv):
    b, lq = qn.shape[:2]
    f32, bf16 = jnp.float32, jnp.bfloat16
    n_pages = page_table.shape[1]
    npg = DECODE_PAGES
    assert n_pages % npg == 0 and lq <= DECODE_NEW_ROWS
    rows = lq * MLA_HEADS
    hd = MLA_HEADS * NOPE_DIM
    eye = jnp.eye(MLA_HEADS, dtype=f32)
    qg = qn.astype(f32) * g_kn.astype(f32)
    qn_bd = (qg[:, :, :, None, :] * eye[None, None, :, :, None]).reshape(b, rows, hd).astype(bf16)
    qr_rows = qr.reshape(b, rows, ROPE_DIM).astype(bf16)
    ind = jnp.tile(jnp.repeat(eye, NOPE_DIM, axis=1), (lq, 1))
    ind = jnp.pad(ind, ((0, -rows % 128), (0, 0))).astype(bf16)
    pad_new = ((0, 0), (0, DECODE_NEW_ROWS - lq), (0, 0))
    cnew = jnp.pad(ckv_new.astype(f32), pad_new)
    krnew = jnp.pad(kr_new.astype(f32), pad_new)

    def page_spec(width, i):
        return pl.BlockSpec((None, PAGE_SIZE, width), lambda bi, j, pt: (pt[bi, j * npg + i], 0, 0))

    def batch_spec(shape):
        return pl.BlockSpec((None,) + shape, lambda bi, j, pt: (bi, 0, 0))

    grid_spec = pltpu.PrefetchScalarGridSpec(
        num_scalar_prefetch=1,
        grid=(b, n_pages // npg),
        in_specs=([page_spec(KV_LORA, i) for i in range(npg)] + [page_spec(ROPE_DIM, i) for i in range(npg)] +
                  [batch_spec((rows, hd)), batch_spec((rows, ROPE_DIM)),
                   batch_spec((DECODE_NEW_ROWS, KV_LORA)), batch_spec((DECODE_NEW_ROWS, ROPE_DIM)),
                   pl.BlockSpec((KV_LORA, hd), lambda bi, j, pt: (0, 0)),
                   pl.BlockSpec(ind.shape, lambda bi, j, pt: (0, 0))]),
        out_specs=batch_spec((rows, KV_LORA)),
        scratch_shapes=[pltpu.VMEM((rows, 1), f32), pltpu.VMEM((rows, 1), f32), pltpu.VMEM((rows, KV_LORA), f32)],
    )
    o_lat = pl.pallas_call(
        _decode_kernel,
        grid_spec=grid_spec,
        out_shape=jax.ShapeDtypeStruct((b, rows, KV_LORA), f32),
        compiler_params=pltpu.CompilerParams(dimension_semantics=("arbitrary", "arbitrary"),
                                             vmem_limit_bytes=48 * 1024 * 1024),
        name="sample_attention",
    )(page_table, *([cache_ckv] * npg), *([cache_krope] * npg), qn_bd, qr_rows, cnew, krnew,
      w_uk.astype(bf16), ind)
    o_lat = o_lat.reshape(b, lq, MLA_HEADS, KV_LORA)
    o = jnp.einsum('bqhc,chd->bqhd', o_lat, w_uv.astype(f32).reshape(KV_LORA, MLA_HEADS, V_DIM))
    return o.reshape(b, lq, MLA_HEADS * V_DIM).astype(qn.dtype)


PEER_ROUTE_TOKENS = 256
PEER_LANE_TILE = 128
PEER_GATHER_TOKENS = 8
PEER_ROW_TILE = (8, 128)
PEER_PAIRS = PEER_HEADS * PEER_TOPK


def _topk_rows(s, k):
    n = s.shape[0]
    iota = lax.broadcasted_iota(jnp.int32, s.shape, 0).astype(jnp.float32)
    vals, idxs = [], []
    for _ in range(k):
        m = jnp.max(s, axis=0, keepdims=True)
        ix = jnp.min(jnp.where(s == m, iota, float(n)), axis=0, keepdims=True)
        vals.append(m)
        idxs.append(ix)
        s = jnp.where(iota == ix, -jnp.inf, s)
    return jnp.concatenate(vals, 0), jnp.concatenate(idxs, 0)


def _select_rows(table, sel):
    out = jnp.zeros(sel.shape, table.dtype)
    for j in range(table.shape[0]):
        out = jnp.where(sel == float(j), table[j:j + 1, :], out)
    return out


def _peer_route_kernel(x_ref, g_ref, wq_ref, gq_ref, k1_ref, k2_ref, xn_ref, idx_ref, gate_ref, xn_scr):
    f32, bf16 = jnp.float32, jnp.bfloat16

    @pl.when(pl.program_id(1) == 0)
    def _():
        x = x_ref[...]
        y = x * lax.rsqrt(jnp.mean(x * x, -1, keepdims=True) + EPS) * g_ref[...]
        xn_ref[...] = y
        xn_scr[...] = y.astype(bf16)

    q = jnp.dot(xn_scr[...], wq_ref[...], preferred_element_type=f32)
    qn = (q * lax.rsqrt(jnp.mean(q * q, -1, keepdims=True) + EPS) * gq_ref[...]).astype(bf16)
    half = PEER_QDIM // 2
    nt = (((1,), (1,)), ((), ()))
    s1 = lax.dot_general(k1_ref[...], qn[:, :half], nt, preferred_element_type=f32)
    s2 = lax.dot_general(k2_ref[...], qn[:, half:], nt, preferred_element_type=f32)
    for c in range(s1.shape[1] // PEER_LANE_TILE):
        cs = slice(c * PEER_LANE_TILE, (c + 1) * PEER_LANE_TILE)
        v1, i1 = _topk_rows(s1[:, cs], PEER_TOPK)
        v2, i2 = _topk_rows(s2[:, cs], PEER_TOPK)
        cand = jnp.concatenate([v1[a:a + 1, :] + v2 for a in range(PEER_TOPK)], 0)
        vals, ci = _topk_rows(cand, PEER_TOPK)
        a = jnp.floor(ci * (1.0 / PEER_TOPK))
        b = ci - a * PEER_TOPK
        e = _select_rows(i1, a) * float(PEER_KEYS) + _select_rows(i2, b)
        ex = jnp.exp(vals - vals[0:1, :])
        idx_ref[:, cs] = e.astype(jnp.int32)
        gate_ref[:, cs] = ex / jnp.sum(ex, axis=0, keepdims=True)


def _peer_route(x, g, w_q, g_q, sub_keys):
    t, d = x.shape
    tb = PEER_ROUTE_TOKENS
    assert t % tb == 0
    bf16 = jnp.bfloat16
    half = PEER_QDIM // 2
    return pl.pallas_call(
        _peer_route_kernel,
        out_shape=(jax.ShapeDtypeStruct((t, d), jnp.float32),
                   jax.ShapeDtypeStruct((PEER_PAIRS, t), jnp.int32),
                   jax.ShapeDtypeStruct((PEER_PAIRS, t), jnp.float32)),
        grid=(t // tb, PEER_HEADS),
        in_specs=[pl.BlockSpec((tb, d), lambda i, h: (i, 0)),
                  pl.BlockSpec((1, d), lambda i, h: (0, 0)),
                  pl.BlockSpec((d, PEER_QDIM), lambda i, h: (0, h)),
                  pl.BlockSpec((1, PEER_QDIM), lambda i, h: (0, 0)),
                  pl.BlockSpec((PEER_KEYS, half), lambda i, h: (0, 0)),
                  pl.BlockSpec((PEER_KEYS, half), lambda i, h: (0, 0))],
        out_specs=(pl.BlockSpec((tb, d), lambda i, h: (i, 0)),
                   pl.BlockSpec((PEER_TOPK, tb), lambda i, h: (h, i)),
                   pl.BlockSpec((PEER_TOPK, tb), lambda i, h: (h, i))),
        scratch_shapes=[pltpu.VMEM((tb, d), bf16)],
        compiler_params=pltpu.CompilerParams(dimension_semantics=("arbitrary", "arbitrary")),
        name="peer_route",
    )(x, g.reshape(1, d), w_q.astype(bf16), g_q.reshape(1, PEER_QDIM),
      sub_keys[0].astype(bf16), sub_keys[1].astype(bf16))


def _peer_expert_kernel(idx_ref, idx_next_ref, x_ref, gate_ref, diag_ref, fold_ref, spread_ref, uv_hbm,
                        out_ref, buf, sems):
    f32, bf16 = jnp.float32, jnp.bfloat16
    tb = PEER_GATHER_TOKENS
    sub = PEER_ROW_TILE[0]
    rows = tb * PEER_PAIRS
    i = pl.program_id(0)
    n = pl.num_programs(0)
    slot = i % 2

    def start_gather(ids_ref, dst_slot):
        def per_token(t, carry):
            base = dst_slot * rows + t * PEER_PAIRS
            for k in range(PEER_PAIRS):
                pltpu.make_async_copy(uv_hbm.at[ids_ref[t, k]], buf.at[base + k], sems.at[dst_slot]).start()
            return carry
        lax.fori_loop(0, tb, per_token, 0)

    @pl.when(i == 0)
    def _():
        start_gather(idx_ref, 0)

    @pl.when(i + 1 < n)
    def _():
        start_gather(idx_next_ref, 1 - slot)

    pltpu.make_async_copy(uv_hbm.at[pl.ds(0, rows)], buf.at[pl.ds(slot * rows, rows)], sems.at[slot]).wait()

    nt = (((1,), (1,)), ((), ()))
    diag = diag_ref[...]
    flat = (PEER_PAIRS * sub, PEER_ROW_TILE[1])
    parts = []
    for t in range(tb):
        ub = buf[pl.ds(slot * rows + t * PEER_PAIRS, PEER_PAIRS), 0:sub, :].reshape(flat).astype(bf16)
        parts.append(lax.dot_general(x_ref[t].astype(bf16), ub, nt, preferred_element_type=f32) * diag)
    e = jnp.concatenate(parts, 0)
    e_hi = e.astype(bf16)
    e_lo = (e - e_hi.astype(f32)).astype(bf16)
    h = jnp.dot(jnp.concatenate([e_hi, e_lo], 0), fold_ref[...], preferred_element_type=f32)
    h = h[:tb * sub] + h[tb * sub:]
    h = jnp.sum(h.reshape(tb, sub, PEER_PAIRS), axis=1)
    act = 0.5 * h * (1.0 + lax.erf(h * math.sqrt(0.5)))
    w = (gate_ref[...] * act).astype(bf16)
    wrep = jnp.broadcast_to(w[:, None, :], (tb, sub, PEER_PAIRS)).reshape(tb * sub, PEER_PAIRS)
    wfull = jnp.dot(wrep, spread_ref[...], preferred_element_type=f32)
    for t in range(tb):
        wm = (wfull[t * sub:(t + 1) * sub] * diag).astype(bf16)
        vb = buf[pl.ds(slot * rows + t * PEER_PAIRS, PEER_PAIRS), sub:2 * sub, :].reshape(flat).astype(bf16)
        out_ref[t] = jnp.dot(wm, vb, preferred_element_type=f32)


def _peer_experts(xn, idx, gates, u_tab, v_tab):
    t, d = xn.shape
    tb = PEER_GATHER_TOKENS
    sub, lanes = PEER_ROW_TILE
    assert t % tb == 0 and d == sub * lanes
    nblk = t // tb
    rows = tb * PEER_PAIRS
    n_exp = u_tab.shape[0]
    uv = jnp.concatenate([u_tab.reshape(n_exp, sub, lanes), v_tab.reshape(n_exp, sub, lanes)], axis=1)
    lane = jnp.arange(d)
    diag = (lane[None, :] % sub == jnp.arange(sub)[:, None]).astype(jnp.float32)
    fold = (lane[:, None] // sub == jnp.arange(PEER_PAIRS)[None, :]).astype(jnp.bfloat16)
    out = pl.pallas_call(
        _peer_expert_kernel,
        out_shape=jax.ShapeDtypeStruct((t, sub, lanes), jnp.float32),
        grid=(nblk,),
        in_specs=[pl.BlockSpec((tb, PEER_PAIRS), lambda i: (i, 0), memory_space=pltpu.SMEM),
                  pl.BlockSpec((tb, PEER_PAIRS), lambda i: (jnp.minimum(i + 1, nblk - 1), 0),
                               memory_space=pltpu.SMEM),
                  pl.BlockSpec((tb, sub, lanes), lambda i: (i, 0, 0)),
                  pl.BlockSpec((tb, PEER_PAIRS), lambda i: (i, 0)),
                  pl.BlockSpec((sub, d), lambda i: (0, 0)),
                  pl.BlockSpec((d, PEER_PAIRS), lambda i: (0, 0)),
                  pl.BlockSpec((PEER_PAIRS, d), lambda i: (0, 0)),
                  pl.BlockSpec(memory_space=pl.ANY)],
        out_specs=pl.BlockSpec((tb, sub, lanes), lambda i: (i, 0, 0)),
        scratch_shapes=[pltpu.VMEM((2 * rows, 2 * sub, lanes), jnp.float32),
                        pltpu.SemaphoreType.DMA((2,))],
        compiler_params=pltpu.CompilerParams(dimension_semantics=("arbitrary",),
                                             vmem_limit_bytes=40 * 1024 * 1024),
        name="peer_experts",
    )(idx, idx, xn.reshape(t, sub, lanes), gates, diag, fold, fold.T, uv)
    return out.reshape(t, d)


def peer(x, g, w_q, g_q, sub_keys, u_tab, v_tab):
    xn, idx_t, gates_t = _peer_route(x, g, w_q, g_q, sub_keys)
    return _peer_experts(xn, idx_t.T, gates_t.T, u_tab, v_tab)


def ple(h, p, g, w_proj, w_gate):
    return (p @ w_proj) * jax.nn.sigmoid(rms_norm(h, g) @ w_gate)


def kernel(x_prompt, x_sample, p_prompt, p_sample, state_ssm, state_conv, cache_ckv, cache_krope, page_table,
           ln_mix, ln_ffn, ln_ple,
           m_w_in, m_conv_w, m_conv_b, m_dt_bias, m_a_log, m_d, m_g_norm, m_w_out,
           kv_g_in, kv_w_dkv, kv_g_ckv, kv_w_kr, kv_g_kr, kv_w_uk, kv_g_kn, kv_w_uv,
           q_w_dq, q_g_cq, q_w_uq, q_g_qn, q_g_qr, a_w_o,
           peer_w_q, peer_g_q, peer_sub_keys, peer_u, peer_v,
           ple_w_proj, ple_w_gate):
    bp, lp = x_prompt.shape[:2]
    pos_p = jnp.arange(lp)
    pos_s = page_table.shape[1] * PAGE_SIZE + jnp.arange(x_sample.shape[1])
    hp, hs = x_prompt, x_sample
    ssm_p, conv_p, ssm_s, conv_s = [], [], [], []
    for i in range(DEPTH):
        if i < N_A:
            mw = (m_w_in[i], m_conv_w[i], m_conv_b[i], m_dt_bias[i], m_a_log[i], m_d[i], m_g_norm[i], m_w_out[i])
            conv0 = jnp.zeros((bp, CONV_W - 1, CONV_DIM), hp.dtype)
            ssm0 = jnp.zeros((bp, SSM_HEADS, SSM_HEADDIM, SSM_STATE), hp.dtype)
            yp, c_p, s_p = mamba_mixer(rms_norm(hp, ln_mix[i]), conv0, ssm0, *mw)
            ys, c_s, s_s = mamba_mixer(rms_norm(hs, ln_mix[i]), state_conv[i], state_ssm[i], *mw)
            conv_p.append(c_p)
            ssm_p.append(s_p)
            conv_s.append(c_s)
            ssm_s.append(s_s)
        else:
            if i == N_A:
                kvw = (kv_g_in, kv_w_dkv, kv_g_ckv, kv_w_kr, kv_g_kr)
                ckv_p, kr_p = mla_latent(hp, pos_p, *kvw)
                ckv_s, kr_s = mla_latent(hs, pos_s, *kvw)
            j = i - N_A
            qw = (q_w_dq[j], q_g_cq[j], q_w_uq[j], q_g_qn[j], q_g_qr[j])
            qn_p, qr_p = mla_queries(rms_norm(hp, ln_mix[i]), pos_p, *qw)
            qn_s, qr_s = mla_queries(rms_norm(hs, ln_mix[i]), pos_s, *qw)
            o_p = mla_attend_prompt(qn_p, qr_p, ckv_p, kr_p, kv_w_uk, kv_g_kn, kv_w_uv)
            yp = jnp.einsum('bhld,hdm->blm', o_p, a_w_o[j].reshape(MLA_HEADS, V_DIM, D_MODEL))
            ys = mla_attend_sample(qn_s, qr_s, ckv_s, kr_s, cache_ckv, cache_krope, page_table,
                                   kv_w_uk, kv_g_kn, kv_w_uv) @ a_w_o[j]
        n_p = bp * lp
        h_all = jnp.concatenate([(hp + yp).reshape(n_p, D_MODEL), (hs + ys).reshape(-1, D_MODEL)], 0)
        h_all = h_all + peer(h_all, ln_ffn[i], peer_w_q[i], peer_g_q[i], peer_sub_keys[i], peer_u[i], peer_v[i])
        hp = h_all[:n_p].reshape(hp.shape)
        hs = h_all[n_p:].reshape(hs.shape)
        hp = hp + ple(hp, p_prompt[i], ln_ple[i], ple_w_proj[i], ple_w_gate[i])
        hs = hs + ple(hs, p_sample[i], ln_ple[i], ple_w_proj[i], ple_w_gate[i])
    return (hp, hs, jnp.stack(ssm_p), jnp.stack(conv_p), ckv_p, kr_p,
            jnp.stack(ssm_s), jnp.stack(conv_s), ckv_s, kr_s)
```

```python
import functools
import math

import jax
import jax.numpy as jnp
from jax import lax
from jax.experimental import pallas as pl
from jax.experimental.pallas import tpu as pltpu

D_MODEL = 1024
DEPTH = 2
PAGE_SIZE = 128
N_A = DEPTH // 2
D_INNER = 2 * D_MODEL
SSM_HEADDIM = 64
SSM_HEADS = D_INNER // SSM_HEADDIM
SSM_GROUPS = 4
SSM_STATE = 128
CONV_W = 4
CONV_DIM = D_INNER + 2 * SSM_GROUPS * SSM_STATE
SSD_CHUNK = 128
MLA_HEADS = 16
Q_LORA = 384
KV_LORA = 256
NOPE_DIM = 64
ROPE_DIM = 32
V_DIM = 64
ROPE_THETA = 10000.0
ATTN_SCALE = (NOPE_DIM + ROPE_DIM) ** -0.5
PEER_HEADS = 8
PEER_KEYS = 128
PEER_QDIM = 256
PEER_TOPK = 16
PLE_DIM = 256
EPS = 1e-6

VMEM_LIMIT = 48 * 1024 * 1024


def rms_norm(x, g):
    xf = x.astype(jnp.float32)
    y = xf * lax.rsqrt(jnp.mean(xf * xf, -1, keepdims=True) + EPS)
    return (y * g.astype(jnp.float32)).astype(x.dtype)


def rope(x, pos):
    half = ROPE_DIM // 2
    inv = ROPE_THETA ** (-jnp.arange(half, dtype=jnp.float32) / half)
    ang = pos.astype(jnp.float32)[:, None] * inv
    ang = ang.reshape((1, ang.shape[0]) + (1,) * (x.ndim - 3) + (half,))
    cos, sin = jnp.cos(ang), jnp.sin(ang)
    xf = x.astype(jnp.float32)
    x1, x2 = xf[..., :half], xf[..., half:]
    return jnp.concatenate([x1 * cos - x2 * sin, x1 * sin + x2 * cos], -1).astype(x.dtype)


DENSE_ROWS = 256


def _dense_kernel(*refs, n_out, has_gain, has_residual):
    refs = list(refs)
    x_ref = refs.pop(0)
    g_ref = refs.pop(0) if has_gain else None
    w_refs = [refs.pop(0) for _ in range(n_out)]
    r_ref = refs.pop(0) if has_residual else None
    x = x_ref[...]
    if has_gain:
        x = x * lax.rsqrt(jnp.mean(x * x, -1, keepdims=True) + EPS) * g_ref[...]
    xb = x.astype(jnp.bfloat16)
    for w_ref, o_ref in zip(w_refs, refs):
        y = jnp.dot(xb, w_ref[...], preferred_element_type=jnp.float32)
        o_ref[...] = y + r_ref[...] if has_residual else y


def _dense(x, weights, gain=None, residual=None):
    rows, k = x.shape
    tm = math.gcd(rows, DENSE_ROWS)
    assert residual is None or len(weights) == 1
    row_spec = lambda n: pl.BlockSpec((tm, n), lambda i: (i, 0))
    full_spec = lambda a: pl.BlockSpec(a.shape, lambda i: (0, 0))
    args, specs = [x], [row_spec(k)]
    if gain is not None:
        args.append(gain.reshape(1, k).astype(jnp.float32))
        specs.append(full_spec(args[-1]))
    for w in weights:
        args.append(w)
        specs.append(full_spec(w))
    if residual is not None:
        args.append(residual)
        specs.append(row_spec(residual.shape[1]))
    return pl.pallas_call(
        functools.partial(_dense_kernel, n_out=len(weights), has_gain=gain is not None,
                          has_residual=residual is not None),
        out_shape=tuple(jax.ShapeDtypeStruct((rows, w.shape[1]), jnp.float32) for w in weights),
        grid=(rows // tm,),
        in_specs=specs,
        out_specs=tuple(row_spec(w.shape[1]) for w in weights),
        compiler_params=pltpu.CompilerParams(dimension_semantics=("arbitrary",), vmem_limit_bytes=VMEM_LIMIT),
        name="dense",
    )(*args)


def _ple_kernel(h_ref, p_ref, g_ref, wp_ref, wg_ref, o_ref):
    f32, bf16 = jnp.float32, jnp.bfloat16
    h = h_ref[...]
    hn = h * lax.rsqrt(jnp.mean(h * h, -1, keepdims=True) + EPS) * g_ref[...]
    gate = jax.nn.sigmoid(jnp.dot(hn.astype(bf16), wg_ref[...], preferred_element_type=f32))
    proj = jnp.dot(p_ref[...].astype(bf16), wp_ref[...], preferred_element_type=f32)
    o_ref[...] = h + proj * gate


def _ple(h, p, g, w_proj, w_gate):
    rows, d = h.shape
    tm = math.gcd(rows, DENSE_ROWS)
    row_spec = lambda n: pl.BlockSpec((tm, n), lambda i: (i, 0))
    full_spec = lambda a: pl.BlockSpec(a.shape, lambda i: (0, 0))
    g2 = g.reshape(1, d).astype(jnp.float32)
    wp, wg = w_proj.astype(jnp.bfloat16), w_gate.astype(jnp.bfloat16)
    return pl.pallas_call(
        _ple_kernel,
        out_shape=jax.ShapeDtypeStruct((rows, d), jnp.float32),
        grid=(rows // tm,),
        in_specs=[row_spec(d), row_spec(p.shape[1]), full_spec(g2), full_spec(wp), full_spec(wg)],
        out_specs=row_spec(d),
        compiler_params=pltpu.CompilerParams(dimension_semantics=("arbitrary",), vmem_limit_bytes=VMEM_LIMIT),
        name="ple",
    )(h, p, g2, wp, wg)


SSD_TAIL = 8


def _ssd_kernel(z_ref, xbc_ref, dt_ref, prev_ref, h0_ref, cw_ref, cb_ref, dtb_ref, a_ref, dskip_ref, gn_ref,
                expand_ref, expand_t_ref, y_ref, hout_ref, ext_scr, h_scr, *, n_valid):
    f32, bf16 = jnp.float32, jnp.bfloat16
    hi = lax.Precision.HIGHEST
    q = z_ref.shape[0]
    c = pl.program_id(1)
    heads_per_group = SSM_HEADS // SSM_GROUPS
    gw = heads_per_group * SSM_HEADDIM
    nt = (((1,), (1,)), ((), ()))
    tn = (((0,), (0,)), ((), ()))

    @pl.when(c == 0)
    def _():
        ext_scr[0:SSD_TAIL, :] = prev_ref[...]
        h_scr[...] = h0_ref[...]

    ext_scr[SSD_TAIL:SSD_TAIL + q, :] = xbc_ref[...]
    conv = cb_ref[...]
    for w in range(CONV_W):
        conv = conv + ext_scr[pl.ds(SSD_TAIL - (CONV_W - 1) + w, q), :] * cw_ref[w:w + 1, :]
    tail = ext_scr[q:q + SSD_TAIL, :]
    ext_scr[0:SSD_TAIL, :] = tail
    act = conv * jax.nn.sigmoid(conv)
    xs = act[:, :D_INNER]
    bmat = act[:, D_INNER:D_INNER + SSM_GROUPS * SSM_STATE].astype(bf16)
    cmat = act[:, D_INNER + SSM_GROUPS * SSM_STATE:].astype(bf16)

    row = lax.broadcasted_iota(jnp.int32, (q, q), 0)
    col = lax.broadcasted_iota(jnp.int32, (q, q), 1)
    dt = jax.nn.softplus(dt_ref[...] + dtb_ref[...])
    if n_valid < q:
        dt = jnp.where(lax.broadcasted_iota(jnp.int32, dt.shape, 0) < n_valid, dt, 0.0)
    a = dt * a_ref[...]
    a_cum = jnp.dot((col <= row).astype(f32), a, precision=hi, preferred_element_type=f32)
    a_cum_t = jnp.dot(a.T, (row <= col).astype(f32), precision=hi, preferred_element_type=f32)
    a_last = a_cum[q - 1:q, :]
    expand = expand_ref[...]
    widen = lambda v: jnp.dot(v, expand, precision=hi, preferred_element_type=f32)
    xd = xs * widen(dt)
    xd_b = xd.astype(bf16)
    xdd_b = (xd * widen(jnp.exp(a_last - a_cum))).astype(bf16)
    grow = widen(jnp.exp(a_cum))
    carry = jnp.dot(expand_t_ref[...], jnp.exp(a_cum_t[:, q - 1:q]), precision=hi,
                    preferred_element_type=f32)

    y_parts = []
    for g in range(SSM_GROUPS):
        bg = bmat[:, g * SSM_STATE:(g + 1) * SSM_STATE]
        cg = cmat[:, g * SSM_STATE:(g + 1) * SSM_STATE]
        cbg = lax.dot_general(cg, bg, nt, preferred_element_type=f32)
        h_in = h_scr[g * gw:(g + 1) * gw, :]
        y_off = lax.dot_general(cg, h_in.astype(bf16), nt, preferred_element_type=f32)
        y_g = y_off * grow[:, g * gw:(g + 1) * gw]
        diag_parts = []
        for r in range(heads_per_group):
            hd = g * heads_per_group + r
            seg = a_cum[:, hd:hd + 1] - a_cum_t[hd:hd + 1, :]
            lmat = jnp.exp(jnp.where(col <= row, seg, -jnp.inf))
            lanes = slice(hd * SSM_HEADDIM, (hd + 1) * SSM_HEADDIM)
            diag_parts.append(jnp.dot((cbg * lmat).astype(bf16), xd_b[:, lanes], preferred_element_type=f32))
        y_parts.append(y_g + jnp.concatenate(diag_parts, 1))
        st = lax.dot_general(xdd_b[:, g * gw:(g + 1) * gw], bg, tn, preferred_element_type=f32)
        h_scr[g * gw:(g + 1) * gw, :] = h_in * carry[g * gw:(g + 1) * gw, :] + st
    y = jnp.concatenate(y_parts, 1) + dskip_ref[...] * xs
    zz = z_ref[...]
    y = y * (zz * jax.nn.sigmoid(zz))
    gsz = D_INNER // SSM_GROUPS
    outs = []
    for g in range(SSM_GROUPS):
        seg = y[:, g * gsz:(g + 1) * gsz]
        outs.append(seg * lax.rsqrt(jnp.mean(seg * seg, -1, keepdims=True) + EPS))
    y_ref[...] = jnp.concatenate(outs, 1) * gn_ref[...]

    @pl.when(c == pl.num_programs(1) - 1)
    def _():
        hout_ref[...] = h_scr[...]


def _ssd(z, xbc, dt, conv_prev, h0, conv_w, conv_b, dt_bias, a_log, d_skip, g_norm, chunk, n_valid):
    b, l, _ = z.shape
    f32 = jnp.float32
    assert l % chunk == 0
    prev = jnp.pad(conv_prev.astype(f32), ((0, 0), (SSD_TAIL - (CONV_W - 1), 0), (0, 0)))
    rows = SSM_HEADS * SSM_HEADDIM
    expand = jnp.repeat(jnp.eye(SSM_HEADS, dtype=f32), SSM_HEADDIM, axis=1)
    vec = lambda v, n: v.astype(f32).reshape(1, n)
    seq_spec = lambda n: pl.BlockSpec((None, chunk, n), lambda bi, ci: (bi, ci, 0))
    bat_spec = lambda r, n: pl.BlockSpec((None, r, n), lambda bi, ci: (bi, 0, 0))
    full_spec = lambda a: pl.BlockSpec(a.shape, lambda bi, ci: (0, 0))
    params = [conv_w.astype(f32), vec(conv_b, CONV_DIM), vec(dt_bias, SSM_HEADS), vec(-jnp.exp(a_log.astype(f32)), SSM_HEADS),
              vec(jnp.repeat(d_skip, SSM_HEADDIM), D_INNER), vec(g_norm, D_INNER), expand, expand.T]
    y, h_new = pl.pallas_call(
        functools.partial(_ssd_kernel, n_valid=n_valid),
        out_shape=(jax.ShapeDtypeStruct((b, l, D_INNER), f32), jax.ShapeDtypeStruct((b, rows, SSM_STATE), f32)),
        grid=(b, l // chunk),
        in_specs=[seq_spec(D_INNER), seq_spec(CONV_DIM), seq_spec(SSM_HEADS), bat_spec(SSD_TAIL, CONV_DIM),
                  bat_spec(rows, SSM_STATE)] + [full_spec(p) for p in params],
        out_specs=(seq_spec(D_INNER), bat_spec(rows, SSM_STATE)),
        scratch_shapes=[pltpu.VMEM((chunk + SSD_TAIL, CONV_DIM), f32), pltpu.VMEM((rows, SSM_STATE), f32)],
        compiler_params=pltpu.CompilerParams(dimension_semantics=("arbitrary", "arbitrary"),
                                             vmem_limit_bytes=VMEM_LIMIT),
        name="ssd",
    )(z, xbc, dt, prev, h0.astype(f32).reshape(b, rows, SSM_STATE), *params)
    return y, h_new.reshape(b, SSM_HEADS, SSM_HEADDIM, SSM_STATE)


SSD_PAD_ROWS = 16


def mamba_sequence(z, xbc, dt, conv_prev, h0, conv_w, conv_b, dt_bias, a_log, d_skip, g_norm):
    b, l, _ = z.shape
    conv_new = jnp.concatenate([conv_prev.astype(xbc.dtype), xbc], 1)[:, -(CONV_W - 1):]
    mw = (conv_w, conv_b, dt_bias, a_log, d_skip, g_norm)
    if l % SSD_CHUNK == 0:
        y, h_new = _ssd(z, xbc, dt, conv_prev, h0, *mw, chunk=SSD_CHUNK, n_valid=SSD_CHUNK)
    else:
        assert l <= SSD_PAD_ROWS
        pad = lambda a: jnp.pad(a, ((0, 0), (0, SSD_PAD_ROWS - l), (0, 0)))
        y, h_new = _ssd(pad(z), pad(xbc), pad(dt), conv_prev, h0, *mw, chunk=SSD_PAD_ROWS, n_valid=l)
        y = y[:, :l]
    return y, conv_new, h_new.astype(z.dtype)


FLASH_TILE = 1024
FLASH_LANES = 128


def _flash_kernel(q_ref, k_ref, v_ref, o_ref, m_scr, l_scr, acc_scr):
    f32 = jnp.float32
    qi, ki = pl.program_id(2), pl.program_id(3)

    @pl.when(ki == 0)
    def _():
        m_scr[...] = jnp.full(m_scr.shape, -jnp.inf, f32)
        l_scr[...] = jnp.zeros(l_scr.shape, f32)
        acc_scr[...] = jnp.zeros(acc_scr.shape, f32)

    def update(on_diagonal):
        s = lax.dot_general(q_ref[...], k_ref[...], (((1,), (1,)), ((), ())), preferred_element_type=f32)
        s = s * ATTN_SCALE
        if on_diagonal:
            row = lax.broadcasted_iota(jnp.int32, s.shape, 0)
            col = lax.broadcasted_iota(jnp.int32, s.shape, 1)
            s = jnp.where(col <= row, s, -jnp.inf)
        m_prev = m_scr[...]
        m_new = jnp.maximum(m_prev, jnp.max(s, axis=-1, keepdims=True))
        alpha = jnp.exp(m_prev - m_new)
        p = jnp.exp(s - m_new)
        l_scr[...] = alpha * l_scr[...] + jnp.sum(p, axis=-1, keepdims=True)
        acc_scr[...] = alpha * acc_scr[...] + jnp.dot(p.astype(v_ref.dtype), v_ref[...],
                                                      preferred_element_type=f32)
        m_scr[...] = m_new

    @pl.when(ki < qi)
    def _():
        update(False)

    @pl.when(ki == qi)
    def _():
        update(True)
        o_ref[...] = acc_scr[...] / l_scr[...]


def _flash_attention(q, k, v):
    b, h, l, _ = q.shape
    t = min(FLASH_TILE, l)
    assert l % t == 0
    nblk = l // t
    return pl.pallas_call(
        _flash_kernel,
        out_shape=jax.ShapeDtypeStruct((b, h, l, V_DIM), jnp.float32),
        grid=(b, h, nblk, nblk),
        in_specs=[pl.BlockSpec((None, None, t, FLASH_LANES), lambda bi, hi, qi, ki: (bi, hi, qi, 0)),
                  pl.BlockSpec((None, None, t, FLASH_LANES),
                               lambda bi, hi, qi, ki: (bi, hi, jnp.minimum(ki, qi), 0)),
                  pl.BlockSpec((None, None, t, V_DIM), lambda bi, hi, qi, ki: (bi, hi, jnp.minimum(ki, qi), 0))],
        out_specs=pl.BlockSpec((None, None, t, V_DIM), lambda bi, hi, qi, ki: (bi, hi, qi, 0)),
        scratch_shapes=[pltpu.VMEM((t, 1), jnp.float32), pltpu.VMEM((t, 1), jnp.float32),
                        pltpu.VMEM((t, V_DIM), jnp.float32)],
        compiler_params=pltpu.CompilerParams(
            dimension_semantics=("arbitrary", "arbitrary", "arbitrary", "arbitrary"),
            vmem_limit_bytes=VMEM_LIMIT),
        name="prompt_attention",
    )(q, k, v)


def mla_attend_prompt(qn, qr, kn, kr, v):
    b, l = qn.shape[:2]
    bf16 = jnp.bfloat16
    pad = jnp.zeros((b, l, MLA_HEADS, FLASH_LANES - NOPE_DIM - ROPE_DIM), bf16)
    q_cat = jnp.concatenate([qn.astype(bf16), qr.astype(bf16), pad], -1).transpose(0, 2, 1, 3)
    kr_h = jnp.broadcast_to(kr.astype(bf16)[:, :, None, :], (b, l, MLA_HEADS, ROPE_DIM))
    k_cat = jnp.concatenate([kn.astype(bf16), kr_h, pad], -1).transpose(0, 2, 1, 3)
    o = _flash_attention(q_cat, k_cat, v.astype(bf16).transpose(0, 2, 1, 3))
    return o.transpose(0, 2, 1, 3).reshape(b, l, MLA_HEADS * V_DIM)


DECODE_PAGES = 4
DECODE_NEW_ROWS = 8


def _decode_kernel(pt_ref, *refs):
    f32, bf16 = jnp.float32, jnp.bfloat16
    npg = DECODE_PAGES
    c_refs, kr_refs = refs[:npg], refs[npg:2 * npg]
    (qn_ref, qr_ref, cnew_ref, krnew_ref, wuk_ref, ind_ref, o_ref, m_scr, l_scr, acc_scr) = refs[2 * npg:]
    j = pl.program_id(1)
    nt = (((1,), (1,)), ((), ()))
    rows = qn_ref.shape[0]

    @pl.when(j == 0)
    def _():
        m_scr[...] = jnp.full(m_scr.shape, -jnp.inf, f32)
        l_scr[...] = jnp.zeros(l_scr.shape, f32)
        acc_scr[...] = jnp.zeros(acc_scr.shape, f32)

    def scores(c, kr):
        cb = c.astype(bf16)
        k = jnp.dot(cb, wuk_ref[...], preferred_element_type=f32)
        ms = lax.dot_general(ind_ref[...], (k * k).astype(bf16), nt, preferred_element_type=f32)
        r = lax.rsqrt(ms[:rows] * (1.0 / NOPE_DIM) + EPS)
        s = r * lax.dot_general(qn_ref[...], k.astype(bf16), nt, preferred_element_type=f32)
        s = s + lax.dot_general(qr_ref[...], kr.astype(bf16), nt, preferred_element_type=f32)
        return s * ATTN_SCALE, cb

    def update(s, cb):
        m_prev = m_scr[...]
        m_new = jnp.maximum(m_prev, jnp.max(s, axis=-1, keepdims=True))
        alpha = jnp.exp(m_prev - m_new)
        p = jnp.exp(s - m_new)
        l_scr[...] = alpha * l_scr[...] + jnp.sum(p, axis=-1, keepdims=True)
        acc_scr[...] = alpha * acc_scr[...] + jnp.dot(p.astype(bf16), cb, preferred_element_type=f32)
        m_scr[...] = m_new

    c_all = jnp.concatenate([r[...] for r in c_refs], 0)
    kr_all = jnp.concatenate([r[...] for r in kr_refs], 0)
    update(*scores(c_all, kr_all))

    @pl.when(j == pl.num_programs(1) - 1)
    def _():
        s, cb = scores(cnew_ref[...], krnew_ref[...])
        key = lax.broadcasted_iota(jnp.int32, s.shape, 1)
        query = lax.broadcasted_iota(jnp.int32, s.shape, 0) // MLA_HEADS
        update(jnp.where(key <= query, s, -jnp.inf), cb)
        o_ref[...] = acc_scr[...] / l_scr[...]


def mla_attend_sample(qn, qr, ckv_new, kr_new, cache_ckv, cache_krope, page_table, w_uk, g_kn, w_uv):
    b, lq = qn.shape[:2]
    f32, bf16 = jnp.float32, jnp.bfloat16
    n_pages = page_table.shape[1]
    npg = DECODE_PAGES
    assert n_pages % npg == 0 and lq <= DECODE_NEW_ROWS
    rows = lq * MLA_HEADS
    hd = MLA_HEADS * NOPE_DIM
    eye = jnp.eye(MLA_HEADS, dtype=f32)
    qg = qn.astype(f32) * g_kn.astype(f32)
    qn_bd = (qg[:, :, :, None, :] * eye[None, None, :, :, None]).reshape(b, rows, hd).astype(bf16)
    qr_rows = qr.reshape(b, rows, ROPE_DIM).astype(bf16)
    ind = jnp.tile(jnp.repeat(eye, NOPE_DIM, axis=1), (lq, 1))
    ind = jnp.pad(ind, ((0, -rows % 128), (0, 0))).astype(bf16)
    pad_new = ((0, 0), (0, DECODE_NEW_ROWS - lq), (0, 0))
    cnew = jnp.pad(ckv_new.astype(f32), pad_new)
    krnew = jnp.pad(kr_new.astype(f32), pad_new)

    def page_spec(width, i):
        return pl.BlockSpec((None, PAGE_SIZE, width), lambda bi, j, pt: (pt[bi, j * npg + i], 0, 0))

    def batch_spec(shape):
        return pl.BlockSpec((None,) + shape, lambda bi, j, pt: (bi, 0, 0))

    grid_spec = pltpu.PrefetchScalarGridSpec(
        num_scalar_prefetch=1,
        grid=(b, n_pages // npg),
        in_specs=([page_spec(KV_LORA, i) for i in range(npg)] + [page_spec(ROPE_DIM, i) for i in range(npg)] +
                  [batch_spec((rows, hd)), batch_spec((rows, ROPE_DIM)),
                   batch_spec((DECODE_NEW_ROWS, KV_LORA)), batch_spec((DECODE_NEW_ROWS, ROPE_DIM)),
                   pl.BlockSpec((KV_LORA, hd), lambda bi, j, pt: (0, 0)),
                   pl.BlockSpec(ind.shape, lambda bi, j, pt: (0, 0))]),
        out_specs=batch_spec((rows, KV_LORA)),
        scratch_shapes=[pltpu.VMEM((rows, 1), f32), pltpu.VMEM((rows, 1), f32), pltpu.VMEM((rows, KV_LORA), f32)],
    )
    o_lat = pl.pallas_call(
        _decode_kernel,
        grid_spec=grid_spec,
        out_shape=jax.ShapeDtypeStruct((b, rows, KV_LORA), f32),
        compiler_params=pltpu.CompilerParams(dimension_semantics=("arbitrary", "arbitrary"),
                                             vmem_limit_bytes=VMEM_LIMIT),
        name="sample_attention",
    )(page_table, *([cache_ckv] * npg), *([cache_krope] * npg), qn_bd, qr_rows, cnew, krnew,
      w_uk.astype(bf16), ind)
    o_lat = o_lat.reshape(b, lq, MLA_HEADS, KV_LORA)
    o = jnp.einsum('bqhc,chd->bqhd', o_lat, w_uv.astype(f32).reshape(KV_LORA, MLA_HEADS, V_DIM))
    return o.reshape(b, lq, MLA_HEADS * V_DIM).astype(qn.dtype)


PEER_ROUTE_TOKENS = 256
PEER_LANE_TILE = 128
PEER_GATHER_TOKENS = 8
PEER_ROW_TILE = (8, 128)
PEER_PAIRS = PEER_HEADS * PEER_TOPK


def _topk_rows(s, k):
    n = s.shape[0]
    iota = lax.broadcasted_iota(jnp.int32, s.shape, 0).astype(jnp.float32)
    vals, idxs = [], []
    for _ in range(k):
        m = jnp.max(s, axis=0, keepdims=True)
        ix = jnp.min(jnp.where(s == m, iota, float(n)), axis=0, keepdims=True)
        vals.append(m)
        idxs.append(ix)
        s = jnp.where(iota == ix, -jnp.inf, s)
    return jnp.concatenate(vals, 0), jnp.concatenate(idxs, 0)


def _select_rows(table, sel):
    out = jnp.zeros(sel.shape, table.dtype)
    for j in range(table.shape[0]):
        out = jnp.where(sel == float(j), table[j:j + 1, :], out)
    return out


def _peer_route_kernel(x_ref, g_ref, wq_ref, gq_ref, k1_ref, k2_ref, xn_ref, idx_ref, gate_ref, xn_scr):
    f32, bf16 = jnp.float32, jnp.bfloat16

    @pl.when(pl.program_id(1) == 0)
    def _():
        x = x_ref[...]
        y = x * lax.rsqrt(jnp.mean(x * x, -1, keepdims=True) + EPS) * g_ref[...]
        xn_ref[...] = y
        xn_scr[...] = y.astype(bf16)

    q = jnp.dot(xn_scr[...], wq_ref[...], preferred_element_type=f32)
    qn = (q * lax.rsqrt(jnp.mean(q * q, -1, keepdims=True) + EPS) * gq_ref[...]).astype(bf16)
    half = PEER_QDIM // 2
    nt = (((1,), (1,)), ((), ()))
    s1 = lax.dot_general(k1_ref[...], qn[:, :half], nt, preferred_element_type=f32)
    s2 = lax.dot_general(k2_ref[...], qn[:, half:], nt, preferred_element_type=f32)
    for c in range(s1.shape[1] // PEER_LANE_TILE):
        cs = slice(c * PEER_LANE_TILE, (c + 1) * PEER_LANE_TILE)
        v1, i1 = _topk_rows(s1[:, cs], PEER_TOPK)
        v2, i2 = _topk_rows(s2[:, cs], PEER_TOPK)
        cand = jnp.concatenate([v1[a:a + 1, :] + v2 for a in range(PEER_TOPK)], 0)
        vals, ci = _topk_rows(cand, PEER_TOPK)
        a = jnp.floor(ci * (1.0 / PEER_TOPK))
        b = ci - a * PEER_TOPK
        e = _select_rows(i1, a) * float(PEER_KEYS) + _select_rows(i2, b)
        ex = jnp.exp(vals - vals[0:1, :])
        idx_ref[:, cs] = e.astype(jnp.int32)
        gate_ref[:, cs] = ex / jnp.sum(ex, axis=0, keepdims=True)


def _peer_route(x, g, w_q, g_q, sub_keys):
    t, d = x.shape
    tb = PEER_ROUTE_TOKENS
    assert t % tb == 0
    bf16 = jnp.bfloat16
    half = PEER_QDIM // 2
    return pl.pallas_call(
        _peer_route_kernel,
        out_shape=(jax.ShapeDtypeStruct((t, d), jnp.float32),
                   jax.ShapeDtypeStruct((PEER_PAIRS, t), jnp.int32),
                   jax.ShapeDtypeStruct((PEER_PAIRS, t), jnp.float32)),
        grid=(t // tb, PEER_HEADS),
        in_specs=[pl.BlockSpec((tb, d), lambda i, h: (i, 0)),
                  pl.BlockSpec((1, d), lambda i, h: (0, 0)),
                  pl.BlockSpec((d, PEER_QDIM), lambda i, h: (0, h)),
                  pl.BlockSpec((1, PEER_QDIM), lambda i, h: (0, 0)),
                  pl.BlockSpec((PEER_KEYS, half), lambda i, h: (0, 0)),
                  pl.BlockSpec((PEER_KEYS, half), lambda i, h: (0, 0))],
        out_specs=(pl.BlockSpec((tb, d), lambda i, h: (i, 0)),
                   pl.BlockSpec((PEER_TOPK, tb), lambda i, h: (h, i)),
                   pl.BlockSpec((PEER_TOPK, tb), lambda i, h: (h, i))),
        scratch_shapes=[pltpu.VMEM((tb, d), bf16)],
        compiler_params=pltpu.CompilerParams(dimension_semantics=("arbitrary", "arbitrary")),
        name="peer_route",
    )(x, g.reshape(1, d), w_q.astype(bf16), g_q.reshape(1, PEER_QDIM),
      sub_keys[0].astype(bf16), sub_keys[1].astype(bf16))


def _peer_expert_kernel(idx_ref, idx_next_ref, x_ref, gate_ref, res_ref, diag_ref, fold_ref, spread_ref, uv_hbm,
                        out_ref, buf, sems):
    f32, bf16 = jnp.float32, jnp.bfloat16
    tb = PEER_GATHER_TOKENS
    sub = PEER_ROW_TILE[0]
    rows = tb * PEER_PAIRS
    i = pl.program_id(0)
    n = pl.num_programs(0)
    slot = i % 2

    def start_gather(ids_ref, dst_slot):
        def per_token(t, carry):
            base = dst_slot * rows + t * PEER_PAIRS
            for k in range(PEER_PAIRS):
                pltpu.make_async_copy(uv_hbm.at[ids_ref[t, k]], buf.at[base + k], sems.at[dst_slot]).start()
            return carry
        lax.fori_loop(0, tb, per_token, 0)

    @pl.when(i == 0)
    def _():
        start_gather(idx_ref, 0)

    @pl.when(i + 1 < n)
    def _():
        start_gather(idx_next_ref, 1 - slot)

    pltpu.make_async_copy(uv_hbm.at[pl.ds(0, rows)], buf.at[pl.ds(slot * rows, rows)], sems.at[slot]).wait()

    nt = (((1,), (1,)), ((), ()))
    diag = diag_ref[...]
    flat = (PEER_PAIRS * sub, PEER_ROW_TILE[1])

    def slab_rows(t, first):
        tok = pl.ds(slot * rows + t * PEER_PAIRS, PEER_PAIRS)
        return buf[tok, first:first + sub, :].reshape(flat).astype(bf16)

    e = jnp.concatenate([lax.dot_general(x_ref[t].astype(bf16), slab_rows(t, 0), nt, preferred_element_type=f32)
                         * diag for t in range(tb)], 0)
    e_hi = e.astype(bf16)
    e_lo = (e - e_hi.astype(f32)).astype(bf16)
    h = jnp.dot(jnp.concatenate([e_hi, e_lo], 0), fold_ref[...], preferred_element_type=f32)
    h = h[:tb * sub] + h[tb * sub:]
    h = jnp.sum(h.reshape(tb, sub, PEER_PAIRS), axis=1)
    act = 0.5 * h * (1.0 + lax.erf(h * math.sqrt(0.5)))
    w = (gate_ref[...] * act).astype(bf16)
    wrep = jnp.broadcast_to(w[:, None, :], (tb, sub, PEER_PAIRS)).reshape(tb * sub, PEER_PAIRS)
    wfull = jnp.dot(wrep, spread_ref[...], preferred_element_type=f32)
    for t in range(tb):
        wm = (wfull[t * sub:(t + 1) * sub] * diag).astype(bf16)
        out_ref[t] = res_ref[t] + jnp.dot(wm, slab_rows(t, sub), preferred_element_type=f32)


def _peer_experts(xn, res, idx, gates, u_tab, v_tab):
    t, d = xn.shape
    tb = PEER_GATHER_TOKENS
    sub, lanes = PEER_ROW_TILE
    assert t % tb == 0 and d == sub * lanes
    nblk = t // tb
    rows = tb * PEER_PAIRS
    n_exp = u_tab.shape[0]
    uv = jnp.concatenate([u_tab.reshape(n_exp, sub, lanes), v_tab.reshape(n_exp, sub, lanes)], axis=1)
    lane = jnp.arange(d)
    diag = (lane[None, :] % sub == jnp.arange(sub)[:, None]).astype(jnp.float32)
    fold = (lane[:, None] // sub == jnp.arange(PEER_PAIRS)[None, :]).astype(jnp.bfloat16)
    tile_spec = pl.BlockSpec((tb, sub, lanes), lambda i: (i, 0, 0))
    full_spec = lambda a: pl.BlockSpec(a.shape, lambda i: (0, 0))
    out = pl.pallas_call(
        _peer_expert_kernel,
        out_shape=jax.ShapeDtypeStruct((t, sub, lanes), jnp.float32),
        grid=(nblk,),
        in_specs=[pl.BlockSpec((tb, PEER_PAIRS), lambda i: (i, 0), memory_space=pltpu.SMEM),
                  pl.BlockSpec((tb, PEER_PAIRS), lambda i: (jnp.minimum(i + 1, nblk - 1), 0),
                               memory_space=pltpu.SMEM),
                  tile_spec,
                  pl.BlockSpec((tb, PEER_PAIRS), lambda i: (i, 0)),
                  tile_spec,
                  full_spec(diag), full_spec(fold), full_spec(fold.T),
                  pl.BlockSpec(memory_space=pl.ANY)],
        out_specs=tile_spec,
        scratch_shapes=[pltpu.VMEM((2 * rows, 2 * sub, lanes), jnp.float32),
                        pltpu.SemaphoreType.DMA((2,))],
        compiler_params=pltpu.CompilerParams(dimension_semantics=("arbitrary",), vmem_limit_bytes=VMEM_LIMIT),
        name="peer_experts",
    )(idx, idx, xn.reshape(t, sub, lanes), gates, res.reshape(t, sub, lanes), diag, fold, fold.T, uv)
    return out.reshape(t, d)


def peer(h, g, w_q, g_q, sub_keys, u_tab, v_tab):
    xn, idx_t, gates_t = _peer_route(h, g, w_q, g_q, sub_keys)
    return _peer_experts(xn, h, idx_t.T, gates_t.T, u_tab, v_tab)


def kernel(x_prompt, x_sample, p_prompt, p_sample, state_ssm, state_conv, cache_ckv, cache_krope, page_table,
           ln_mix, ln_ffn, ln_ple,
           m_w_in, m_conv_w, m_conv_b, m_dt_bias, m_a_log, m_d, m_g_norm, m_w_out,
           kv_g_in, kv_w_dkv, kv_g_ckv, kv_w_kr, kv_g_kr, kv_w_uk, kv_g_kn, kv_w_uv,
           q_w_dq, q_g_cq, q_w_uq, q_g_qn, q_g_qr, a_w_o,
           peer_w_q, peer_g_q, peer_sub_keys, peer_u, peer_v,
           ple_w_proj, ple_w_gate):
    bf16 = jnp.bfloat16
    bp, lp = x_prompt.shape[:2]
    bs, ls = x_sample.shape[:2]
    n_p, n_s = bp * lp, bs * ls
    pos_p = jnp.arange(lp)
    pos_s = page_table.shape[1] * PAGE_SIZE + jnp.arange(ls)
    h = jnp.concatenate([x_prompt.reshape(n_p, D_MODEL), x_sample.reshape(n_s, D_MODEL)], 0)
    p_all = jnp.concatenate([p_prompt.reshape(DEPTH, n_p, PLE_DIM), p_sample.reshape(DEPTH, n_s, PLE_DIM)], 1)

    def split(a, tail):
        return a[:n_p].reshape((bp, lp) + tail), a[n_p:].reshape((bs, ls) + tail)

    ssm_p, conv_p, ssm_s, conv_s = [], [], [], []
    for i in range(DEPTH):
        if i < N_A:
            w_in = m_w_in[i].astype(bf16)
            z, xbc, dt = _dense(h, [w_in[:, :D_INNER], w_in[:, D_INNER:D_INNER + CONV_DIM],
                                    w_in[:, D_INNER + CONV_DIM:]], gain=ln_mix[i])
            mw = (m_conv_w[i], m_conv_b[i], m_dt_bias[i], m_a_log[i], m_d[i], m_g_norm[i])
            (z_p, z_s), (xbc_p, xbc_s), (dt_p, dt_s) = (split(z, (D_INNER,)), split(xbc, (CONV_DIM,)),
                                                        split(dt, (SSM_HEADS,)))
            conv0 = jnp.zeros((bp, CONV_W - 1, CONV_DIM), h.dtype)
            ssm0 = jnp.zeros((bp, SSM_HEADS, SSM_HEADDIM, SSM_STATE), h.dtype)
            y_p, c_p, s_p = mamba_sequence(z_p, xbc_p, dt_p, conv0, ssm0, *mw)
            y_s, c_s, s_s = mamba_sequence(z_s, xbc_s, dt_s, state_conv[i], state_ssm[i], *mw)
            conv_p.append(c_p)
            ssm_p.append(s_p)
            conv_s.append(c_s)
            ssm_s.append(s_s)
            y = jnp.concatenate([y_p.reshape(n_p, D_INNER), y_s.reshape(n_s, D_INNER)], 0)
            (h,) = _dense(y, [m_w_out[i].astype(bf16)], residual=h)
        else:
            if i == N_A:
                ckv_raw, kr_raw = _dense(h, [kv_w_dkv.astype(bf16), kv_w_kr.astype(bf16)], gain=kv_g_in)
                ckv = rms_norm(ckv_raw, kv_g_ckv)
                ckv_p, ckv_s = split(ckv, (KV_LORA,))
                kr_p, kr_s = split(rms_norm(kr_raw, kv_g_kr), (ROPE_DIM,))
                kr_p, kr_s = rope(kr_p, pos_p), rope(kr_s, pos_s)
                k_raw, v_p = _dense(ckv[:n_p], [kv_w_uk.astype(bf16), kv_w_uv.astype(bf16)])
                kn_p = rms_norm(k_raw.reshape(bp, lp, MLA_HEADS, NOPE_DIM), kv_g_kn)
                v_p = v_p.reshape(bp, lp, MLA_HEADS, V_DIM)
            j = i - N_A
            (cq_raw,) = _dense(h, [q_w_dq[j].astype(bf16)], gain=ln_mix[i])
            (q,) = _dense(cq_raw, [q_w_uq[j].astype(bf16)], gain=q_g_cq[j])
            q = q.reshape(n_p + n_s, MLA_HEADS, NOPE_DIM + ROPE_DIM)
            qn_p, qn_s = split(rms_norm(q[..., :NOPE_DIM], q_g_qn[j]), (MLA_HEADS, NOPE_DIM))
            qr_p, qr_s = split(rms_norm(q[..., NOPE_DIM:], q_g_qr[j]), (MLA_HEADS, ROPE_DIM))
            qr_p, qr_s = rope(qr_p, pos_p), rope(qr_s, pos_s)
            o_p = mla_attend_prompt(qn_p, qr_p, kn_p, kr_p, v_p)
            o_s = mla_attend_sample(qn_s, qr_s, ckv_s, kr_s, cache_ckv, cache_krope, page_table,
                                    kv_w_uk, kv_g_kn, kv_w_uv)
            o = jnp.concatenate([o_p.reshape(n_p, MLA_HEADS * V_DIM), o_s.reshape(n_s, MLA_HEADS * V_DIM)], 0)
            (h,) = _dense(o, [a_w_o[j].astype(bf16)], residual=h)
        h = peer(h, ln_ffn[i], peer_w_q[i], peer_g_q[i], peer_sub_keys[i], peer_u[i], peer_v[i])
        h = _ple(h, p_all[i], ln_ple[i], ple_w_proj[i], ple_w_gate[i])
    hp, hs = split(h, (D_MODEL,))
    return (hp, hs, jnp.stack(ssm_p), jnp.stack(conv_p), ckv_p, kr_p,
            jnp.stack(ssm_s), jnp.stack(conv_s), ckv_s, kr_s)
```

```python
import functools
import math

import jax
import jax.numpy as jnp
from jax import lax
from jax.experimental import pallas as pl
from jax.experimental.pallas import tpu as pltpu

D_MODEL = 1024
DEPTH = 2
PAGE_SIZE = 128
N_A = DEPTH // 2
D_INNER = 2 * D_MODEL
SSM_HEADDIM = 64
SSM_HEADS = D_INNER // SSM_HEADDIM
SSM_GROUPS = 4
SSM_STATE = 128
CONV_W = 4
CONV_DIM = D_INNER + 2 * SSM_GROUPS * SSM_STATE
SSD_CHUNK = 128
MLA_HEADS = 16
Q_LORA = 384
KV_LORA = 256
NOPE_DIM = 64
ROPE_DIM = 32
V_DIM = 64
ROPE_THETA = 10000.0
ATTN_SCALE = (NOPE_DIM + ROPE_DIM) ** -0.5
PEER_HEADS = 8
PEER_KEYS = 128
PEER_QDIM = 256
PEER_TOPK = 16
PLE_DIM = 256
EPS = 1e-6

VMEM_LIMIT = 48 * 1024 * 1024


def rms_norm(x, g):
    xf = x.astype(jnp.float32)
    y = xf * lax.rsqrt(jnp.mean(xf * xf, -1, keepdims=True) + EPS)
    return (y * g.astype(jnp.float32)).astype(x.dtype)


def rope(x, pos):
    half = ROPE_DIM // 2
    inv = ROPE_THETA ** (-jnp.arange(half, dtype=jnp.float32) / half)
    ang = pos.astype(jnp.float32)[:, None] * inv
    ang = ang.reshape((1, ang.shape[0]) + (1,) * (x.ndim - 3) + (half,))
    cos, sin = jnp.cos(ang), jnp.sin(ang)
    xf = x.astype(jnp.float32)
    x1, x2 = xf[..., :half], xf[..., half:]
    return jnp.concatenate([x1 * cos - x2 * sin, x1 * sin + x2 * cos], -1).astype(x.dtype)


DENSE_ROWS = 256


def _dense_kernel(*refs, n_out, has_gain, has_residual):
    refs = list(refs)
    x_ref = refs.pop(0)
    g_ref = refs.pop(0) if has_gain else None
    w_refs = [refs.pop(0) for _ in range(n_out)]
    r_ref = refs.pop(0) if has_residual else None
    x = x_ref[...]
    if has_gain:
        x = x * lax.rsqrt(jnp.mean(x * x, -1, keepdims=True) + EPS) * g_ref[...]
    xb = x.astype(jnp.bfloat16)
    for w_ref, o_ref in zip(w_refs, refs):
        y = jnp.dot(xb, w_ref[...], preferred_element_type=jnp.float32)
        o_ref[...] = y + r_ref[...] if has_residual else y


def _dense(x, weights, gain=None, residual=None):
    rows, k = x.shape
    tm = math.gcd(rows, DENSE_ROWS)
    assert residual is None or len(weights) == 1
    row_spec = lambda n: pl.BlockSpec((tm, n), lambda i: (i, 0))
    full_spec = lambda a: pl.BlockSpec(a.shape, lambda i: (0, 0))
    args, specs = [x], [row_spec(k)]
    if gain is not None:
        args.append(gain.reshape(1, k).astype(jnp.float32))
        specs.append(full_spec(args[-1]))
    for w in weights:
        args.append(w)
        specs.append(full_spec(w))
    if residual is not None:
        args.append(residual)
        specs.append(row_spec(residual.shape[1]))
    return pl.pallas_call(
        functools.partial(_dense_kernel, n_out=len(weights), has_gain=gain is not None,
                          has_residual=residual is not None),
        out_shape=tuple(jax.ShapeDtypeStruct((rows, w.shape[1]), jnp.float32) for w in weights),
        grid=(rows // tm,),
        in_specs=specs,
        out_specs=tuple(row_spec(w.shape[1]) for w in weights),
        compiler_params=pltpu.CompilerParams(dimension_semantics=("arbitrary",), vmem_limit_bytes=VMEM_LIMIT),
        name="dense",
    )(*args)


def _ple_kernel(h_ref, p_ref, g_ref, wp_ref, wg_ref, o_ref):
    f32, bf16 = jnp.float32, jnp.bfloat16
    h = h_ref[...]
    hn = h * lax.rsqrt(jnp.mean(h * h, -1, keepdims=True) + EPS) * g_ref[...]
    gate = jax.nn.sigmoid(jnp.dot(hn.astype(bf16), wg_ref[...], preferred_element_type=f32))
    proj = jnp.dot(p_ref[...].astype(bf16), wp_ref[...], preferred_element_type=f32)
    o_ref[...] = h + proj * gate


def _ple(h, p, g, w_proj, w_gate):
    rows, d = h.shape
    tm = math.gcd(rows, DENSE_ROWS)
    row_spec = lambda n: pl.BlockSpec((tm, n), lambda i: (i, 0))
    full_spec = lambda a: pl.BlockSpec(a.shape, lambda i: (0, 0))
    g2 = g.reshape(1, d).astype(jnp.float32)
    wp, wg = w_proj.astype(jnp.bfloat16), w_gate.astype(jnp.bfloat16)
    return pl.pallas_call(
        _ple_kernel,
        out_shape=jax.ShapeDtypeStruct((rows, d), jnp.float32),
        grid=(rows // tm,),
        in_specs=[row_spec(d), row_spec(p.shape[1]), full_spec(g2), full_spec(wp), full_spec(wg)],
        out_specs=row_spec(d),
        compiler_params=pltpu.CompilerParams(dimension_semantics=("arbitrary",), vmem_limit_bytes=VMEM_LIMIT),
        name="ple",
    )(h, p, g2, wp, wg)


SSD_TAIL = 8


def _ssd_kernel(z_ref, xbc_ref, dt_ref, prev_ref, h0_ref, cw_ref, cb_ref, dtb_ref, a_ref, dskip_ref, gn_ref,
                expand_ref, expand_t_ref, y_ref, hout_ref, ext_scr, h_scr, *, n_valid):
    f32, bf16 = jnp.float32, jnp.bfloat16
    hi = lax.Precision.HIGHEST
    q = z_ref.shape[0]
    c = pl.program_id(1)
    heads_per_group = SSM_HEADS // SSM_GROUPS
    gw = heads_per_group * SSM_HEADDIM
    nt = (((1,), (1,)), ((), ()))
    tn = (((0,), (0,)), ((), ()))

    @pl.when(c == 0)
    def _():
        ext_scr[0:SSD_TAIL, :] = prev_ref[...]
        h_scr[...] = h0_ref[...]

    ext_scr[SSD_TAIL:SSD_TAIL + q, :] = xbc_ref[...]
    conv = cb_ref[...]
    for w in range(CONV_W):
        conv = conv + ext_scr[pl.ds(SSD_TAIL - (CONV_W - 1) + w, q), :] * cw_ref[w:w + 1, :]
    tail = ext_scr[q:q + SSD_TAIL, :]
    ext_scr[0:SSD_TAIL, :] = tail
    act = conv * jax.nn.sigmoid(conv)
    xs = act[:, :D_INNER]
    bmat = act[:, D_INNER:D_INNER + SSM_GROUPS * SSM_STATE].astype(bf16)
    cmat = act[:, D_INNER + SSM_GROUPS * SSM_STATE:].astype(bf16)

    row = lax.broadcasted_iota(jnp.int32, (q, q), 0)
    col = lax.broadcasted_iota(jnp.int32, (q, q), 1)
    dt = jax.nn.softplus(dt_ref[...] + dtb_ref[...])
    if n_valid < q:
        dt = jnp.where(lax.broadcasted_iota(jnp.int32, dt.shape, 0) < n_valid, dt, 0.0)
    a = dt * a_ref[...]
    a_cum = jnp.dot((col <= row).astype(f32), a, precision=hi, preferred_element_type=f32)
    a_cum_t = jnp.dot(a.T, (row <= col).astype(f32), precision=hi, preferred_element_type=f32)
    a_last = a_cum[q - 1:q, :]
    expand = expand_ref[...]
    widen = lambda v: jnp.dot(v, expand, precision=hi, preferred_element_type=f32)
    xd = xs * widen(dt)
    xd_b = xd.astype(bf16)
    xdd_b = (xd * widen(jnp.exp(a_last - a_cum))).astype(bf16)
    grow = widen(jnp.exp(a_cum))
    carry = jnp.dot(expand_t_ref[...], jnp.exp(a_cum_t[:, q - 1:q]), precision=hi,
                    preferred_element_type=f32)

    y_parts = []
    for g in range(SSM_GROUPS):
        bg = bmat[:, g * SSM_STATE:(g + 1) * SSM_STATE]
        cg = cmat[:, g * SSM_STATE:(g + 1) * SSM_STATE]
        cbg = lax.dot_general(cg, bg, nt, preferred_element_type=f32)
        h_in = h_scr[g * gw:(g + 1) * gw, :]
        y_off = lax.dot_general(cg, h_in.astype(bf16), nt, preferred_element_type=f32)
        y_g = y_off * grow[:, g * gw:(g + 1) * gw]
        diag_parts = []
        for r in range(heads_per_group):
            hd = g * heads_per_group + r
            seg = a_cum[:, hd:hd + 1] - a_cum_t[hd:hd + 1, :]
            lmat = jnp.exp(jnp.where(col <= row, seg, -jnp.inf))
            lanes = slice(hd * SSM_HEADDIM, (hd + 1) * SSM_HEADDIM)
            diag_parts.append(jnp.dot((cbg * lmat).astype(bf16), xd_b[:, lanes], preferred_element_type=f32))
        y_parts.append(y_g + jnp.concatenate(diag_parts, 1))
        st = lax.dot_general(xdd_b[:, g * gw:(g + 1) * gw], bg, tn, preferred_element_type=f32)
        h_scr[g * gw:(g + 1) * gw, :] = h_in * carry[g * gw:(g + 1) * gw, :] + st
    y = jnp.concatenate(y_parts, 1) + dskip_ref[...] * xs
    zz = z_ref[...]
    y = y * (zz * jax.nn.sigmoid(zz))
    gsz = D_INNER // SSM_GROUPS
    outs = []
    for g in range(SSM_GROUPS):
        seg = y[:, g * gsz:(g + 1) * gsz]
        outs.append(seg * lax.rsqrt(jnp.mean(seg * seg, -1, keepdims=True) + EPS))
    y_ref[...] = jnp.concatenate(outs, 1) * gn_ref[...]

    @pl.when(c == pl.num_programs(1) - 1)
    def _():
        hout_ref[...] = h_scr[...]


def _ssd(z, xbc, dt, conv_prev, h0, conv_w, conv_b, dt_bias, a_log, d_skip, g_norm, chunk, n_valid):
    b, l, _ = z.shape
    f32 = jnp.float32
    assert l % chunk == 0
    prev = jnp.pad(conv_prev.astype(f32), ((0, 0), (SSD_TAIL - (CONV_W - 1), 0), (0, 0)))
    rows = SSM_HEADS * SSM_HEADDIM
    expand = jnp.repeat(jnp.eye(SSM_HEADS, dtype=f32), SSM_HEADDIM, axis=1)
    vec = lambda v, n: v.astype(f32).reshape(1, n)
    seq_spec = lambda n: pl.BlockSpec((None, chunk, n), lambda bi, ci: (bi, ci, 0))
    bat_spec = lambda r, n: pl.BlockSpec((None, r, n), lambda bi, ci: (bi, 0, 0))
    full_spec = lambda a: pl.BlockSpec(a.shape, lambda bi, ci: (0, 0))
    params = [conv_w.astype(f32), vec(conv_b, CONV_DIM), vec(dt_bias, SSM_HEADS), vec(-jnp.exp(a_log.astype(f32)), SSM_HEADS),
              vec(jnp.repeat(d_skip, SSM_HEADDIM), D_INNER), vec(g_norm, D_INNER), expand, expand.T]
    y, h_new = pl.pallas_call(
        functools.partial(_ssd_kernel, n_valid=n_valid),
        out_shape=(jax.ShapeDtypeStruct((b, l, D_INNER), f32), jax.ShapeDtypeStruct((b, rows, SSM_STATE), f32)),
        grid=(b, l // chunk),
        in_specs=[seq_spec(D_INNER), seq_spec(CONV_DIM), seq_spec(SSM_HEADS), bat_spec(SSD_TAIL, CONV_DIM),
                  bat_spec(rows, SSM_STATE)] + [full_spec(p) for p in params],
        out_specs=(seq_spec(D_INNER), bat_spec(rows, SSM_STATE)),
        scratch_shapes=[pltpu.VMEM((chunk + SSD_TAIL, CONV_DIM), f32), pltpu.VMEM((rows, SSM_STATE), f32)],
        compiler_params=pltpu.CompilerParams(dimension_semantics=("arbitrary", "arbitrary"),
                                             vmem_limit_bytes=VMEM_LIMIT),
        name="ssd",
    )(z, xbc, dt, prev, h0.astype(f32).reshape(b, rows, SSM_STATE), *params)
    return y, h_new.reshape(b, SSM_HEADS, SSM_HEADDIM, SSM_STATE)


SSD_PAD_ROWS = 16


def mamba_sequence(z, xbc, dt, conv_prev, h0, conv_w, conv_b, dt_bias, a_log, d_skip, g_norm):
    b, l, _ = z.shape
    conv_new = jnp.concatenate([conv_prev.astype(xbc.dtype), xbc], 1)[:, -(CONV_W - 1):]
    mw = (conv_w, conv_b, dt_bias, a_log, d_skip, g_norm)
    if l % SSD_CHUNK == 0:
        y, h_new = _ssd(z, xbc, dt, conv_prev, h0, *mw, chunk=SSD_CHUNK, n_valid=SSD_CHUNK)
    else:
        assert l <= SSD_PAD_ROWS
        pad = lambda a: jnp.pad(a, ((0, 0), (0, SSD_PAD_ROWS - l), (0, 0)))
        y, h_new = _ssd(pad(z), pad(xbc), pad(dt), conv_prev, h0, *mw, chunk=SSD_PAD_ROWS, n_valid=l)
        y = y[:, :l]
    return y, conv_new, h_new.astype(z.dtype)


FLASH_TILE = 1024
FLASH_LANES = 128


def _flash_kernel(q_ref, k_ref, vt_ref, o_ref, m_scr, l_scr, acc_scr):
    f32 = jnp.float32
    qi, ki = pl.program_id(2), pl.program_id(3)

    @pl.when(ki == 0)
    def _():
        m_scr[...] = jnp.full(m_scr.shape, -jnp.inf, f32)
        l_scr[...] = jnp.zeros(l_scr.shape, f32)
        acc_scr[...] = jnp.zeros(acc_scr.shape, f32)

    def update(on_diagonal):
        s = lax.dot_general(k_ref[...], q_ref[...], (((1,), (1,)), ((), ())), preferred_element_type=f32)
        s = s * ATTN_SCALE
        if on_diagonal:
            key = lax.broadcasted_iota(jnp.int32, s.shape, 0)
            query = lax.broadcasted_iota(jnp.int32, s.shape, 1)
            s = jnp.where(key <= query, s, -jnp.inf)
        m_prev = m_scr[...]
        m_new = jnp.maximum(m_prev, jnp.max(s, axis=0, keepdims=True))
        alpha = jnp.exp(m_prev - m_new)
        p = jnp.exp(s - m_new)
        l_scr[...] = alpha * l_scr[...] + jnp.sum(p, axis=0, keepdims=True)
        acc_scr[...] = alpha * acc_scr[...] + jnp.dot(vt_ref[...], p.astype(vt_ref.dtype),
                                                      preferred_element_type=f32)
        m_scr[...] = m_new

    @pl.when(ki < qi)
    def _():
        update(False)

    @pl.when(ki == qi)
    def _():
        update(True)
        o_ref[...] = acc_scr[...] / l_scr[...]


def _flash_attention(q, k, vt):
    b, h, l, _ = q.shape
    t = min(FLASH_TILE, l)
    assert l % t == 0
    nblk = l // t
    return pl.pallas_call(
        _flash_kernel,
        out_shape=jax.ShapeDtypeStruct((b, h, V_DIM, l), jnp.float32),
        grid=(b, h, nblk, nblk),
        in_specs=[pl.BlockSpec((None, None, t, FLASH_LANES), lambda bi, hi, qi, ki: (bi, hi, qi, 0)),
                  pl.BlockSpec((None, None, t, FLASH_LANES),
                               lambda bi, hi, qi, ki: (bi, hi, jnp.minimum(ki, qi), 0)),
                  pl.BlockSpec((None, None, V_DIM, t), lambda bi, hi, qi, ki: (bi, hi, 0, jnp.minimum(ki, qi)))],
        out_specs=pl.BlockSpec((None, None, V_DIM, t), lambda bi, hi, qi, ki: (bi, hi, 0, qi)),
        scratch_shapes=[pltpu.VMEM((1, t), jnp.float32), pltpu.VMEM((1, t), jnp.float32),
                        pltpu.VMEM((V_DIM, t), jnp.float32)],
        compiler_params=pltpu.CompilerParams(
            dimension_semantics=("arbitrary", "arbitrary", "arbitrary", "arbitrary"),
            vmem_limit_bytes=VMEM_LIMIT),
        name="prompt_attention",
    )(q, k, vt)


def mla_attend_prompt(qn, qr, kn, kr, v):
    b, l = qn.shape[:2]
    bf16 = jnp.bfloat16
    pad = jnp.zeros((b, l, MLA_HEADS, FLASH_LANES - NOPE_DIM - ROPE_DIM), bf16)
    q_cat = jnp.concatenate([qn.astype(bf16), qr.astype(bf16), pad], -1).transpose(0, 2, 1, 3)
    kr_h = jnp.broadcast_to(kr.astype(bf16)[:, :, None, :], (b, l, MLA_HEADS, ROPE_DIM))
    k_cat = jnp.concatenate([kn.astype(bf16), kr_h, pad], -1).transpose(0, 2, 1, 3)
    o = _flash_attention(q_cat, k_cat, v.astype(bf16).transpose(0, 2, 3, 1))
    return o.transpose(0, 3, 1, 2).reshape(b, l, MLA_HEADS * V_DIM)


DECODE_PAGES = 8
DECODE_NEW_ROWS = 8


def _decode_kernel(pt_ref, *refs):
    f32, bf16 = jnp.float32, jnp.bfloat16
    npg = DECODE_PAGES
    c_refs, kr_refs = refs[:npg], refs[npg:2 * npg]
    (qn_ref, qr_ref, cnew_ref, krnew_ref, wuk_ref, ind_ref, o_ref, m_scr, l_scr, acc_scr) = refs[2 * npg:]
    j = pl.program_id(1)
    nt = (((1,), (1,)), ((), ()))
    rows = qn_ref.shape[0]

    @pl.when(j == 0)
    def _():
        m_scr[...] = jnp.full(m_scr.shape, -jnp.inf, f32)
        l_scr[...] = jnp.zeros(l_scr.shape, f32)
        acc_scr[...] = jnp.zeros(acc_scr.shape, f32)

    def scores(c, kr):
        cb = c.astype(bf16)
        k = jnp.dot(cb, wuk_ref[...], preferred_element_type=f32)
        ms = lax.dot_general(ind_ref[...], (k * k).astype(bf16), nt, preferred_element_type=f32)
        r = lax.rsqrt(ms[:rows] * (1.0 / NOPE_DIM) + EPS)
        s = r * lax.dot_general(qn_ref[...], k.astype(bf16), nt, preferred_element_type=f32)
        s = s + lax.dot_general(qr_ref[...], kr.astype(bf16), nt, preferred_element_type=f32)
        return s * ATTN_SCALE, cb

    def update(s, cb):
        m_prev = m_scr[...]
        m_new = jnp.maximum(m_prev, jnp.max(s, axis=-1, keepdims=True))
        alpha = jnp.exp(m_prev - m_new)
        p = jnp.exp(s - m_new)
        l_scr[...] = alpha * l_scr[...] + jnp.sum(p, axis=-1, keepdims=True)
        acc_scr[...] = alpha * acc_scr[...] + jnp.dot(p.astype(bf16), cb, preferred_element_type=f32)
        m_scr[...] = m_new

    c_all = jnp.concatenate([r[...] for r in c_refs], 0)
    kr_all = jnp.concatenate([r[...] for r in kr_refs], 0)
    update(*scores(c_all, kr_all))

    @pl.when(j == pl.num_programs(1) - 1)
    def _():
        s, cb = scores(cnew_ref[...], krnew_ref[...])
        key = lax.broadcasted_iota(jnp.int32, s.shape, 1)
        query = lax.broadcasted_iota(jnp.int32, s.shape, 0) // MLA_HEADS
        update(jnp.where(key <= query, s, -jnp.inf), cb)
        o_ref[...] = acc_scr[...] / l_scr[...]


def mla_attend_sample(qn, qr, ckv_new, kr_new, cache_ckv, cache_krope, page_table, w_uk, g_kn, w_uv):
    b, lq = qn.shape[:2]
    f32, bf16 = jnp.float32, jnp.bfloat16
    n_pages = page_table.shape[1]
    npg = DECODE_PAGES
    assert n_pages % npg == 0 and lq <= DECODE_NEW_ROWS
    rows = lq * MLA_HEADS
    hd = MLA_HEADS * NOPE_DIM
    eye = jnp.eye(MLA_HEADS, dtype=f32)
    qg = qn.astype(f32) * g_kn.astype(f32)
    qn_bd = (qg[:, :, :, None, :] * eye[None, None, :, :, None]).reshape(b, rows, hd).astype(bf16)
    qr_rows = qr.reshape(b, rows, ROPE_DIM).astype(bf16)
    ind = jnp.tile(jnp.repeat(eye, NOPE_DIM, axis=1), (lq, 1))
    ind = jnp.pad(ind, ((0, -rows % 128), (0, 0))).astype(bf16)
    pad_new = ((0, 0), (0, DECODE_NEW_ROWS - lq), (0, 0))
    cnew = jnp.pad(ckv_new.astype(f32), pad_new)
    krnew = jnp.pad(kr_new.astype(f32), pad_new)

    def page_spec(width, i):
        return pl.BlockSpec((None, PAGE_SIZE, width), lambda bi, j, pt: (pt[bi, j * npg + i], 0, 0))

    def batch_spec(shape):
        return pl.BlockSpec((None,) + shape, lambda bi, j, pt: (bi, 0, 0))

    grid_spec = pltpu.PrefetchScalarGridSpec(
        num_scalar_prefetch=1,
        grid=(b, n_pages // npg),
        in_specs=([page_spec(KV_LORA, i) for i in range(npg)] + [page_spec(ROPE_DIM, i) for i in range(npg)] +
                  [batch_spec((rows, hd)), batch_spec((rows, ROPE_DIM)),
                   batch_spec((DECODE_NEW_ROWS, KV_LORA)), batch_spec((DECODE_NEW_ROWS, ROPE_DIM)),
                   pl.BlockSpec((KV_LORA, hd), lambda bi, j, pt: (0, 0)),
                   pl.BlockSpec(ind.shape, lambda bi, j, pt: (0, 0))]),
        out_specs=batch_spec((rows, KV_LORA)),
        scratch_shapes=[pltpu.VMEM((rows, 1), f32), pltpu.VMEM((rows, 1), f32), pltpu.VMEM((rows, KV_LORA), f32)],
    )
    o_lat = pl.pallas_call(
        _decode_kernel,
        grid_spec=grid_spec,
        out_shape=jax.ShapeDtypeStruct((b, rows, KV_LORA), f32),
        compiler_params=pltpu.CompilerParams(dimension_semantics=("arbitrary", "arbitrary"),
                                             vmem_limit_bytes=VMEM_LIMIT),
        name="sample_attention",
    )(page_table, *([cache_ckv] * npg), *([cache_krope] * npg), qn_bd, qr_rows, cnew, krnew,
      w_uk.astype(bf16), ind)
    o_lat = o_lat.reshape(b, lq, MLA_HEADS, KV_LORA)
    o = jnp.einsum('bqhc,chd->bqhd', o_lat, w_uv.astype(f32).reshape(KV_LORA, MLA_HEADS, V_DIM))
    return o.reshape(b, lq, MLA_HEADS * V_DIM).astype(qn.dtype)


PEER_ROUTE_TOKENS = 256
PEER_LANE_TILE = 128
PEER_GATHER_TOKENS = 8
PEER_ROW_TILE = (8, 128)
PEER_PAIRS = PEER_HEADS * PEER_TOPK


def _topk_rows(s, k):
    n = s.shape[0]
    iota = lax.broadcasted_iota(jnp.int32, s.shape, 0).astype(jnp.float32)
    vals, idxs = [], []
    for _ in range(k):
        m = jnp.max(s, axis=0, keepdims=True)
        ix = jnp.min(jnp.where(s == m, iota, float(n)), axis=0, keepdims=True)
        vals.append(m)
        idxs.append(ix)
        s = jnp.where(iota == ix, -jnp.inf, s)
    return jnp.concatenate(vals, 0), jnp.concatenate(idxs, 0)


def _select_rows(table, sel):
    out = jnp.zeros(sel.shape, table.dtype)
    for j in range(table.shape[0]):
        out = jnp.where(sel == float(j), table[j:j + 1, :], out)
    return out


def _peer_route_kernel(x_ref, g_ref, wq_ref, gq_ref, k1_ref, k2_ref, xn_ref, idx_ref, gate_ref, xn_scr):
    f32, bf16 = jnp.float32, jnp.bfloat16

    @pl.when(pl.program_id(1) == 0)
    def _():
        x = x_ref[...]
        y = x * lax.rsqrt(jnp.mean(x * x, -1, keepdims=True) + EPS) * g_ref[...]
        xn_ref[...] = y
        xn_scr[...] = y.astype(bf16)

    q = jnp.dot(xn_scr[...], wq_ref[...], preferred_element_type=f32)
    qn = (q * lax.rsqrt(jnp.mean(q * q, -1, keepdims=True) + EPS) * gq_ref[...]).astype(bf16)
    half = PEER_QDIM // 2
    nt = (((1,), (1,)), ((), ()))
    s1 = lax.dot_general(k1_ref[...], qn[:, :half], nt, preferred_element_type=f32)
    s2 = lax.dot_general(k2_ref[...], qn[:, half:], nt, preferred_element_type=f32)
    for c in range(s1.shape[1] // PEER_LANE_TILE):
        cs = slice(c * PEER_LANE_TILE, (c + 1) * PEER_LANE_TILE)
        v1, i1 = _topk_rows(s1[:, cs], PEER_TOPK)
        v2, i2 = _topk_rows(s2[:, cs], PEER_TOPK)
        cand = jnp.concatenate([v1[a:a + 1, :] + v2 for a in range(PEER_TOPK)], 0)
        vals, ci = _topk_rows(cand, PEER_TOPK)
        a = jnp.floor(ci * (1.0 / PEER_TOPK))
        b = ci - a * PEER_TOPK
        e = _select_rows(i1, a) * float(PEER_KEYS) + _select_rows(i2, b)
        ex = jnp.exp(vals - vals[0:1, :])
        idx_ref[:, cs] = e.astype(jnp.int32)
        gate_ref[:, cs] = ex / jnp.sum(ex, axis=0, keepdims=True)


def _peer_route(x, g, w_q, g_q, sub_keys):
    t, d = x.shape
    tb = PEER_ROUTE_TOKENS
    assert t % tb == 0
    bf16 = jnp.bfloat16
    half = PEER_QDIM // 2
    return pl.pallas_call(
        _peer_route_kernel,
        out_shape=(jax.ShapeDtypeStruct((t, d), jnp.float32),
                   jax.ShapeDtypeStruct((PEER_PAIRS, t), jnp.int32),
                   jax.ShapeDtypeStruct((PEER_PAIRS, t), jnp.float32)),
        grid=(t // tb, PEER_HEADS),
        in_specs=[pl.BlockSpec((tb, d), lambda i, h: (i, 0)),
                  pl.BlockSpec((1, d), lambda i, h: (0, 0)),
                  pl.BlockSpec((d, PEER_QDIM), lambda i, h: (0, h)),
                  pl.BlockSpec((1, PEER_QDIM), lambda i, h: (0, 0)),
                  pl.BlockSpec((PEER_KEYS, half), lambda i, h: (0, 0)),
                  pl.BlockSpec((PEER_KEYS, half), lambda i, h: (0, 0))],
        out_specs=(pl.BlockSpec((tb, d), lambda i, h: (i, 0)),
                   pl.BlockSpec((PEER_TOPK, tb), lambda i, h: (h, i)),
                   pl.BlockSpec((PEER_TOPK, tb), lambda i, h: (h, i))),
        scratch_shapes=[pltpu.VMEM((tb, d), bf16)],
        compiler_params=pltpu.CompilerParams(dimension_semantics=("arbitrary", "arbitrary")),
        name="peer_route",
    )(x, g.reshape(1, d), w_q.astype(bf16), g_q.reshape(1, PEER_QDIM),
      sub_keys[0].astype(bf16), sub_keys[1].astype(bf16))


def _peer_expert_kernel(idx_ref, idx_next_ref, x_ref, gate_ref, res_ref, diag_ref, fold_ref, spread_ref, uv_hbm,
                        out_ref, buf_even, buf_odd, sems):
    f32, bf16 = jnp.float32, jnp.bfloat16
    tb = PEER_GATHER_TOKENS
    sub = PEER_ROW_TILE[0]
    rows = tb * PEER_PAIRS
    i = pl.program_id(0)
    n = pl.num_programs(0)

    def start_gather(ids_ref, dst, sem):
        for t in range(tb):
            for k in range(PEER_PAIRS):
                pltpu.make_async_copy(uv_hbm.at[ids_ref[t, k]], dst.at[t * PEER_PAIRS + k], sem).start()

    def wait_gather(dst, sem):
        pltpu.make_async_copy(uv_hbm.at[pl.ds(0, rows)], dst, sem).wait()

    def mix(cur):
        nt = (((1,), (1,)), ((), ()))
        diag = diag_ref[...]
        flat = (PEER_PAIRS * sub, PEER_ROW_TILE[1])

        def slab_rows(t, first):
            return cur[t * PEER_PAIRS:(t + 1) * PEER_PAIRS, first:first + sub, :].reshape(flat).astype(bf16)

        e = jnp.concatenate([lax.dot_general(x_ref[t].astype(bf16), slab_rows(t, 0), nt,
                                             preferred_element_type=f32) * diag for t in range(tb)], 0)
        e_hi = e.astype(bf16)
        e_lo = (e - e_hi.astype(f32)).astype(bf16)
        h = jnp.dot(jnp.concatenate([e_hi, e_lo], 0), fold_ref[...], preferred_element_type=f32)
        h = h[:tb * sub] + h[tb * sub:]
        h = jnp.sum(h.reshape(tb, sub, PEER_PAIRS), axis=1)
        act = 0.5 * h * (1.0 + lax.erf(h * math.sqrt(0.5)))
        w = (gate_ref[...] * act).astype(bf16)
        wrep = jnp.broadcast_to(w[:, None, :], (tb, sub, PEER_PAIRS)).reshape(tb * sub, PEER_PAIRS)
        wfull = jnp.dot(wrep, spread_ref[...], preferred_element_type=f32)
        for t in range(tb):
            wm = (wfull[t * sub:(t + 1) * sub] * diag).astype(bf16)
            out_ref[t] = res_ref[t] + jnp.dot(wm, slab_rows(t, sub), preferred_element_type=f32)

    @pl.when(i == 0)
    def _():
        start_gather(idx_ref, buf_even, sems.at[0])

    def step(cur, cur_sem, nxt, nxt_sem):
        wait_gather(cur, cur_sem)
        start_gather(idx_next_ref, nxt, nxt_sem)
        mix(cur)

        @pl.when(i == n - 1)
        def _():
            wait_gather(nxt, nxt_sem)

    @pl.when(i % 2 == 0)
    def _():
        step(buf_even, sems.at[0], buf_odd, sems.at[1])

    @pl.when(i % 2 == 1)
    def _():
        step(buf_odd, sems.at[1], buf_even, sems.at[0])


def _peer_experts(xn, res, idx, gates, u_tab, v_tab):
    t, d = xn.shape
    tb = PEER_GATHER_TOKENS
    sub, lanes = PEER_ROW_TILE
    assert t % tb == 0 and d == sub * lanes
    nblk = t // tb
    rows = tb * PEER_PAIRS
    n_exp = u_tab.shape[0]
    uv = jnp.concatenate([u_tab.reshape(n_exp, sub, lanes), v_tab.reshape(n_exp, sub, lanes)], axis=1)
    lane = jnp.arange(d)
    diag = (lane[None, :] % sub == jnp.arange(sub)[:, None]).astype(jnp.float32)
    fold = (lane[:, None] // sub == jnp.arange(PEER_PAIRS)[None, :]).astype(jnp.bfloat16)
    tile_spec = pl.BlockSpec((tb, sub, lanes), lambda i: (i, 0, 0))
    full_spec = lambda a: pl.BlockSpec(a.shape, lambda i: (0, 0))
    out = pl.pallas_call(
        _peer_expert_kernel,
        out_shape=jax.ShapeDtypeStruct((t, sub, lanes), jnp.float32),
        grid=(nblk,),
        in_specs=[pl.BlockSpec((tb, PEER_PAIRS), lambda i: (i, 0), memory_space=pltpu.SMEM),
                  pl.BlockSpec((tb, PEER_PAIRS), lambda i: (jnp.minimum(i + 1, nblk - 1), 0),
                               memory_space=pltpu.SMEM),
                  tile_spec,
                  pl.BlockSpec((tb, PEER_PAIRS), lambda i: (i, 0)),
                  tile_spec,
                  full_spec(diag), full_spec(fold), full_spec(fold.T),
                  pl.BlockSpec(memory_space=pl.ANY)],
        out_specs=tile_spec,
        scratch_shapes=[pltpu.VMEM((rows, 2 * sub, lanes), jnp.float32),
                        pltpu.VMEM((rows, 2 * sub, lanes), jnp.float32),
                        pltpu.SemaphoreType.DMA((2,))],
        compiler_params=pltpu.CompilerParams(dimension_semantics=("arbitrary",), vmem_limit_bytes=VMEM_LIMIT),
        name="peer_experts",
    )(idx, idx, xn.reshape(t, sub, lanes), gates, res.reshape(t, sub, lanes), diag, fold, fold.T, uv)
    return out.reshape(t, d)


def peer(h, g, w_q, g_q, sub_keys, u_tab, v_tab):
    xn, idx_t, gates_t = _peer_route(h, g, w_q, g_q, sub_keys)
    return _peer_experts(xn, h, idx_t.T, gates_t.T, u_tab, v_tab)


def kernel(x_prompt, x_sample, p_prompt, p_sample, state_ssm, state_conv, cache_ckv, cache_krope, page_table,
           ln_mix, ln_ffn, ln_ple,
           m_w_in, m_conv_w, m_conv_b, m_dt_bias, m_a_log, m_d, m_g_norm, m_w_out,
           kv_g_in, kv_w_dkv, kv_g_ckv, kv_w_kr, kv_g_kr, kv_w_uk, kv_g_kn, kv_w_uv,
           q_w_dq, q_g_cq, q_w_uq, q_g_qn, q_g_qr, a_w_o,
           peer_w_q, peer_g_q, peer_sub_keys, peer_u, peer_v,
           ple_w_proj, ple_w_gate):
    bf16 = jnp.bfloat16
    bp, lp = x_prompt.shape[:2]
    bs, ls = x_sample.shape[:2]
    n_p, n_s = bp * lp, bs * ls
    pos_p = jnp.arange(lp)
    pos_s = page_table.shape[1] * PAGE_SIZE + jnp.arange(ls)
    h = jnp.concatenate([x_prompt.reshape(n_p, D_MODEL), x_sample.reshape(n_s, D_MODEL)], 0)
    p_all = jnp.concatenate([p_prompt.reshape(DEPTH, n_p, PLE_DIM), p_sample.reshape(DEPTH, n_s, PLE_DIM)], 1)

    def split(a, tail):
        return a[:n_p].reshape((bp, lp) + tail), a[n_p:].reshape((bs, ls) + tail)

    ssm_p, conv_p, ssm_s, conv_s = [], [], [], []
    for i in range(DEPTH):
        if i < N_A:
            w_in = m_w_in[i].astype(bf16)
            z, xbc, dt = _dense(h, [w_in[:, :D_INNER], w_in[:, D_INNER:D_INNER + CONV_DIM],
                                    w_in[:, D_INNER + CONV_DIM:]], gain=ln_mix[i])
            mw = (m_conv_w[i], m_conv_b[i], m_dt_bias[i], m_a_log[i], m_d[i], m_g_norm[i])
            (z_p, z_s), (xbc_p, xbc_s), (dt_p, dt_s) = (split(z, (D_INNER,)), split(xbc, (CONV_DIM,)),
                                                        split(dt, (SSM_HEADS,)))
            conv0 = jnp.zeros((bp, CONV_W - 1, CONV_DIM), h.dtype)
            ssm0 = jnp.zeros((bp, SSM_HEADS, SSM_HEADDIM, SSM_STATE), h.dtype)
            y_p, c_p, s_p = mamba_sequence(z_p, xbc_p, dt_p, conv0, ssm0, *mw)
            y_s, c_s, s_s = mamba_sequence(z_s, xbc_s, dt_s, state_conv[i], state_ssm[i], *mw)
            conv_p.append(c_p)
            ssm_p.append(s_p)
            conv_s.append(c_s)
            ssm_s.append(s_s)
            y = jnp.concatenate([y_p.reshape(n_p, D_INNER), y_s.reshape(n_s, D_INNER)], 0)
            (h,) = _dense(y, [m_w_out[i].astype(bf16)], residual=h)
        else:
            if i == N_A:
                ckv_raw, kr_raw = _dense(h, [kv_w_dkv.astype(bf16), kv_w_kr.astype(bf16)], gain=kv_g_in)
                ckv = rms_norm(ckv_raw, kv_g_ckv)
                ckv_p, ckv_s = split(ckv, (KV_LORA,))
                kr_p, kr_s = split(rms_norm(kr_raw, kv_g_kr), (ROPE_DIM,))
                kr_p, kr_s = rope(kr_p, pos_p), rope(kr_s, pos_s)
                k_raw, v_p = _dense(ckv[:n_p], [kv_w_uk.astype(bf16), kv_w_uv.astype(bf16)])
                kn_p = rms_norm(k_raw.reshape(bp, lp, MLA_HEADS, NOPE_DIM), kv_g_kn)
                v_p = v_p.reshape(bp, lp, MLA_HEADS, V_DIM)
            j = i - N_A
            (cq_raw,) = _dense(h, [q_w_dq[j].astype(bf16)], gain=ln_mix[i])
            (q,) = _dense(cq_raw, [q_w_uq[j].astype(bf16)], gain=q_g_cq[j])
            q = q.reshape(n_p + n_s, MLA_HEADS, NOPE_DIM + ROPE_DIM)
            qn_p, qn_s = split(rms_norm(q[..., :NOPE_DIM], q_g_qn[j]), (MLA_HEADS, NOPE_DIM))
            qr_p, qr_s = split(rms_norm(q[..., NOPE_DIM:], q_g_qr[j]), (MLA_HEADS, ROPE_DIM))
            qr_p, qr_s = rope(qr_p, pos_p), rope(qr_s, pos_s)
            o_p = mla_attend_prompt(qn_p, qr_p, kn_p, kr_p, v_p)
            o_s = mla_attend_sample(qn_s, qr_s, ckv_s, kr_s, cache_ckv, cache_krope, page_table,
                                    kv_w_uk, kv_g_kn, kv_w_uv)
            o = jnp.concatenate([o_p.reshape(n_p, MLA_HEADS * V_DIM), o_s.reshape(n_s, MLA_HEADS * V_DIM)], 0)
            (h,) = _dense(o, [a_w_o[j].astype(bf16)], residual=h)
        h = peer(h, ln_ffn[i], peer_w_q[i], peer_g_q[i], peer_sub_keys[i], peer_u[i], peer_v[i])
        h = _ple(h, p_all[i], ln_ple[i], ple_w_proj[i], ple_w_gate[i])
    hp, hs = split(h, (D_MODEL,))
    return (hp, hs, jnp.stack(ssm_p), jnp.stack(conv_p), ckv_p, kr_p,
            jnp.stack(ssm_s), jnp.stack(conv_s), ckv_s, kr_s)
```

```python
import functools
import math

import jax
import jax.numpy as jnp
from jax import lax
from jax.experimental import pallas as pl
from jax.experimental.pallas import tpu as pltpu

D_MODEL = 1024
DEPTH = 2
PAGE_SIZE = 128
N_A = DEPTH // 2
D_INNER = 2 * D_MODEL
SSM_HEADDIM = 64
SSM_HEADS = D_INNER // SSM_HEADDIM
SSM_GROUPS = 4
SSM_STATE = 128
CONV_W = 4
CONV_DIM = D_INNER + 2 * SSM_GROUPS * SSM_STATE
SSD_CHUNK = 128
MLA_HEADS = 16
Q_LORA = 384
KV_LORA = 256
NOPE_DIM = 64
ROPE_DIM = 32
V_DIM = 64
ROPE_THETA = 10000.0
ATTN_SCALE = (NOPE_DIM + ROPE_DIM) ** -0.5
PEER_HEADS = 8
PEER_KEYS = 128
PEER_QDIM = 256
PEER_TOPK = 16
PLE_DIM = 256
EPS = 1e-6

VMEM_LIMIT = 48 * 1024 * 1024


def rms_norm(x, g):
    xf = x.astype(jnp.float32)
    y = xf * lax.rsqrt(jnp.mean(xf * xf, -1, keepdims=True) + EPS)
    return (y * g.astype(jnp.float32)).astype(x.dtype)


def rope(x, pos):
    half = ROPE_DIM // 2
    inv = ROPE_THETA ** (-jnp.arange(half, dtype=jnp.float32) / half)
    ang = pos.astype(jnp.float32)[:, None] * inv
    ang = ang.reshape((1, ang.shape[0]) + (1,) * (x.ndim - 3) + (half,))
    cos, sin = jnp.cos(ang), jnp.sin(ang)
    xf = x.astype(jnp.float32)
    x1, x2 = xf[..., :half], xf[..., half:]
    return jnp.concatenate([x1 * cos - x2 * sin, x1 * sin + x2 * cos], -1).astype(x.dtype)


DENSE_ROWS = 256


def _dense_kernel(*refs, n_out, has_gain, has_residual):
    refs = list(refs)
    x_ref = refs.pop(0)
    g_ref = refs.pop(0) if has_gain else None
    w_refs = [refs.pop(0) for _ in range(n_out)]
    r_ref = refs.pop(0) if has_residual else None
    x = x_ref[...]
    if has_gain:
        x = x * lax.rsqrt(jnp.mean(x * x, -1, keepdims=True) + EPS) * g_ref[...]
    xb = x.astype(jnp.bfloat16)
    for w_ref, o_ref in zip(w_refs, refs):
        y = jnp.dot(xb, w_ref[...], preferred_element_type=jnp.float32)
        o_ref[...] = y + r_ref[...] if has_residual else y


def _dense(x, weights, gain=None, residual=None):
    rows, k = x.shape
    tm = math.gcd(rows, DENSE_ROWS)
    assert residual is None or len(weights) == 1
    row_spec = lambda n: pl.BlockSpec((tm, n), lambda i: (i, 0))
    full_spec = lambda a: pl.BlockSpec(a.shape, lambda i: (0, 0))
    args, specs = [x], [row_spec(k)]
    if gain is not None:
        args.append(gain.reshape(1, k).astype(jnp.float32))
        specs.append(full_spec(args[-1]))
    for w in weights:
        args.append(w)
        specs.append(full_spec(w))
    if residual is not None:
        args.append(residual)
        specs.append(row_spec(residual.shape[1]))
    return pl.pallas_call(
        functools.partial(_dense_kernel, n_out=len(weights), has_gain=gain is not None,
                          has_residual=residual is not None),
        out_shape=tuple(jax.ShapeDtypeStruct((rows, w.shape[1]), jnp.float32) for w in weights),
        grid=(rows // tm,),
        in_specs=specs,
        out_specs=tuple(row_spec(w.shape[1]) for w in weights),
        compiler_params=pltpu.CompilerParams(dimension_semantics=("arbitrary",), vmem_limit_bytes=VMEM_LIMIT),
        name="dense",
    )(*args)


def _ple_kernel(h_ref, p_ref, g_ref, wp_ref, wg_ref, o_ref):
    f32, bf16 = jnp.float32, jnp.bfloat16
    h = h_ref[...]
    hn = h * lax.rsqrt(jnp.mean(h * h, -1, keepdims=True) + EPS) * g_ref[...]
    gate = jax.nn.sigmoid(jnp.dot(hn.astype(bf16), wg_ref[...], preferred_element_type=f32))
    proj = jnp.dot(p_ref[...].astype(bf16), wp_ref[...], preferred_element_type=f32)
    o_ref[...] = h + proj * gate


def _ple(h, p, g, w_proj, w_gate):
    rows, d = h.shape
    tm = math.gcd(rows, DENSE_ROWS)
    row_spec = lambda n: pl.BlockSpec((tm, n), lambda i: (i, 0))
    full_spec = lambda a: pl.BlockSpec(a.shape, lambda i: (0, 0))
    g2 = g.reshape(1, d).astype(jnp.float32)
    wp, wg = w_proj.astype(jnp.bfloat16), w_gate.astype(jnp.bfloat16)
    return pl.pallas_call(
        _ple_kernel,
        out_shape=jax.ShapeDtypeStruct((rows, d), jnp.float32),
        grid=(rows // tm,),
        in_specs=[row_spec(d), row_spec(p.shape[1]), full_spec(g2), full_spec(wp), full_spec(wg)],
        out_specs=row_spec(d),
        compiler_params=pltpu.CompilerParams(dimension_semantics=("arbitrary",), vmem_limit_bytes=VMEM_LIMIT),
        name="ple",
    )(h, p, g2, wp, wg)


SSD_TAIL = 8


def _ssd_kernel(z_ref, xbc_ref, dt_ref, prev_ref, h0_ref, cw_ref, cb_ref, dtb_ref, a_ref, dskip_ref, gn_ref,
                expand_ref, expand_t_ref, y_ref, hout_ref, ext_scr, h_scr, *, n_valid):
    f32, bf16 = jnp.float32, jnp.bfloat16
    hi = lax.Precision.HIGHEST
    q = z_ref.shape[0]
    c = pl.program_id(1)
    heads_per_group = SSM_HEADS // SSM_GROUPS
    gw = heads_per_group * SSM_HEADDIM
    nt = (((1,), (1,)), ((), ()))
    tn = (((0,), (0,)), ((), ()))

    @pl.when(c == 0)
    def _():
        ext_scr[0:SSD_TAIL, :] = prev_ref[...]
        h_scr[...] = h0_ref[...]

    ext_scr[SSD_TAIL:SSD_TAIL + q, :] = xbc_ref[...]
    conv = cb_ref[...]
    for w in range(CONV_W):
        conv = conv + ext_scr[pl.ds(SSD_TAIL - (CONV_W - 1) + w, q), :] * cw_ref[w:w + 1, :]
    tail = ext_scr[q:q + SSD_TAIL, :]
    ext_scr[0:SSD_TAIL, :] = tail
    act = conv * jax.nn.sigmoid(conv)
    xs = act[:, :D_INNER]
    bmat = act[:, D_INNER:D_INNER + SSM_GROUPS * SSM_STATE].astype(bf16)
    cmat = act[:, D_INNER + SSM_GROUPS * SSM_STATE:].astype(bf16)

    row = lax.broadcasted_iota(jnp.int32, (q, q), 0)
    col = lax.broadcasted_iota(jnp.int32, (q, q), 1)
    dt = jax.nn.softplus(dt_ref[...] + dtb_ref[...])
    if n_valid < q:
        dt = jnp.where(lax.broadcasted_iota(jnp.int32, dt.shape, 0) < n_valid, dt, 0.0)
    a = dt * a_ref[...]
    a_cum = jnp.dot((col <= row).astype(f32), a, precision=hi, preferred_element_type=f32)
    a_cum_t = jnp.dot(a.T, (row <= col).astype(f32), precision=hi, preferred_element_type=f32)
    a_last = a_cum[q - 1:q, :]
    expand = expand_ref[...]
    widen = lambda v: jnp.dot(v, expand, precision=hi, preferred_element_type=f32)
    xd = xs * widen(dt)
    xd_b = xd.astype(bf16)
    xdd_b = (xd * widen(jnp.exp(a_last - a_cum))).astype(bf16)
    grow = widen(jnp.exp(a_cum))
    carry = jnp.dot(expand_t_ref[...], jnp.exp(a_cum_t[:, q - 1:q]), precision=hi,
                    preferred_element_type=f32)

    y_parts = []
    for g in range(SSM_GROUPS):
        bg = bmat[:, g * SSM_STATE:(g + 1) * SSM_STATE]
        cg = cmat[:, g * SSM_STATE:(g + 1) * SSM_STATE]
        cbg = lax.dot_general(cg, bg, nt, preferred_element_type=f32)
        h_in = h_scr[g * gw:(g + 1) * gw, :]
        y_off = lax.dot_general(cg, h_in.astype(bf16), nt, preferred_element_type=f32)
        y_g = y_off * grow[:, g * gw:(g + 1) * gw]
        diag_parts = []
        for r in range(heads_per_group):
            hd = g * heads_per_group + r
            seg = a_cum[:, hd:hd + 1] - a_cum_t[hd:hd + 1, :]
            lmat = jnp.exp(jnp.where(col <= row, seg, -jnp.inf))
            lanes = slice(hd * SSM_HEADDIM, (hd + 1) * SSM_HEADDIM)
            diag_parts.append(jnp.dot((cbg * lmat).astype(bf16), xd_b[:, lanes], preferred_element_type=f32))
        y_parts.append(y_g + jnp.concatenate(diag_parts, 1))
        st = lax.dot_general(xdd_b[:, g * gw:(g + 1) * gw], bg, tn, preferred_element_type=f32)
        h_scr[g * gw:(g + 1) * gw, :] = h_in * carry[g * gw:(g + 1) * gw, :] + st
    y = jnp.concatenate(y_parts, 1) + dskip_ref[...] * xs
    zz = z_ref[...]
    y = y * (zz * jax.nn.sigmoid(zz))
    gsz = D_INNER // SSM_GROUPS
    outs = []
    for g in range(SSM_GROUPS):
        seg = y[:, g * gsz:(g + 1) * gsz]
        outs.append(seg * lax.rsqrt(jnp.mean(seg * seg, -1, keepdims=True) + EPS))
    y_ref[...] = jnp.concatenate(outs, 1) * gn_ref[...]

    @pl.when(c == pl.num_programs(1) - 1)
    def _():
        hout_ref[...] = h_scr[...]


def _ssd(z, xbc, dt, conv_prev, h0, conv_w, conv_b, dt_bias, a_log, d_skip, g_norm, chunk, n_valid):
    b, l, _ = z.shape
    f32 = jnp.float32
    assert l % chunk == 0
    prev = jnp.pad(conv_prev.astype(f32), ((0, 0), (SSD_TAIL - (CONV_W - 1), 0), (0, 0)))
    rows = SSM_HEADS * SSM_HEADDIM
    expand = jnp.repeat(jnp.eye(SSM_HEADS, dtype=f32), SSM_HEADDIM, axis=1)
    vec = lambda v, n: v.astype(f32).reshape(1, n)
    seq_spec = lambda n: pl.BlockSpec((None, chunk, n), lambda bi, ci: (bi, ci, 0))
    bat_spec = lambda r, n: pl.BlockSpec((None, r, n), lambda bi, ci: (bi, 0, 0))
    full_spec = lambda a: pl.BlockSpec(a.shape, lambda bi, ci: (0, 0))
    params = [conv_w.astype(f32), vec(conv_b, CONV_DIM), vec(dt_bias, SSM_HEADS), vec(-jnp.exp(a_log.astype(f32)), SSM_HEADS),
              vec(jnp.repeat(d_skip, SSM_HEADDIM), D_INNER), vec(g_norm, D_INNER), expand, expand.T]
    y, h_new = pl.pallas_call(
        functools.partial(_ssd_kernel, n_valid=n_valid),
        out_shape=(jax.ShapeDtypeStruct((b, l, D_INNER), f32), jax.ShapeDtypeStruct((b, rows, SSM_STATE), f32)),
        grid=(b, l // chunk),
        in_specs=[seq_spec(D_INNER), seq_spec(CONV_DIM), seq_spec(SSM_HEADS), bat_spec(SSD_TAIL, CONV_DIM),
                  bat_spec(rows, SSM_STATE)] + [full_spec(p) for p in params],
        out_specs=(seq_spec(D_INNER), bat_spec(rows, SSM_STATE)),
        scratch_shapes=[pltpu.VMEM((chunk + SSD_TAIL, CONV_DIM), f32), pltpu.VMEM((rows, SSM_STATE), f32)],
        compiler_params=pltpu.CompilerParams(dimension_semantics=("arbitrary", "arbitrary"),
                                             vmem_limit_bytes=VMEM_LIMIT),
        name="ssd",
    )(z, xbc, dt, prev, h0.astype(f32).reshape(b, rows, SSM_STATE), *params)
    return y, h_new.reshape(b, SSM_HEADS, SSM_HEADDIM, SSM_STATE)


SSD_PAD_ROWS = 16


def mamba_sequence(z, xbc, dt, conv_prev, h0, conv_w, conv_b, dt_bias, a_log, d_skip, g_norm):
    b, l, _ = z.shape
    conv_new = jnp.concatenate([conv_prev.astype(xbc.dtype), xbc], 1)[:, -(CONV_W - 1):]
    mw = (conv_w, conv_b, dt_bias, a_log, d_skip, g_norm)
    if l % SSD_CHUNK == 0:
        y, h_new = _ssd(z, xbc, dt, conv_prev, h0, *mw, chunk=SSD_CHUNK, n_valid=SSD_CHUNK)
    else:
        assert l <= SSD_PAD_ROWS
        pad = lambda a: jnp.pad(a, ((0, 0), (0, SSD_PAD_ROWS - l), (0, 0)))
        y, h_new = _ssd(pad(z), pad(xbc), pad(dt), conv_prev, h0, *mw, chunk=SSD_PAD_ROWS, n_valid=l)
        y = y[:, :l]
    return y, conv_new, h_new.astype(z.dtype)


FLASH_TILE = 1024
FLASH_LANES = 128


def _flash_kernel(q_ref, k_ref, vt_ref, o_ref, m_scr, l_scr, acc_scr):
    f32 = jnp.float32
    qi, ki = pl.program_id(2), pl.program_id(3)

    @pl.when(ki == 0)
    def _():
        m_scr[...] = jnp.full(m_scr.shape, -jnp.inf, f32)
        l_scr[...] = jnp.zeros(l_scr.shape, f32)
        acc_scr[...] = jnp.zeros(acc_scr.shape, f32)

    def update(on_diagonal):
        s = lax.dot_general(k_ref[...], q_ref[...], (((1,), (1,)), ((), ())), preferred_element_type=f32)
        s = s * ATTN_SCALE
        if on_diagonal:
            key = lax.broadcasted_iota(jnp.int32, s.shape, 0)
            query = lax.broadcasted_iota(jnp.int32, s.shape, 1)
            s = jnp.where(key <= query, s, -jnp.inf)
        m_prev = m_scr[...]
        m_new = jnp.maximum(m_prev, jnp.max(s, axis=0, keepdims=True))
        alpha = jnp.exp(m_prev - m_new)
        p = jnp.exp(s - m_new)
        l_scr[...] = alpha * l_scr[...] + jnp.sum(p, axis=0, keepdims=True)
        acc_scr[...] = alpha * acc_scr[...] + jnp.dot(vt_ref[...], p.astype(vt_ref.dtype),
                                                      preferred_element_type=f32)
        m_scr[...] = m_new

    @pl.when(ki < qi)
    def _():
        update(False)

    @pl.when(ki == qi)
    def _():
        update(True)
        o_ref[...] = acc_scr[...] / l_scr[...]


def _flash_attention(q, k, vt):
    b, h, l, _ = q.shape
    t = min(FLASH_TILE, l)
    assert l % t == 0
    nblk = l // t
    return pl.pallas_call(
        _flash_kernel,
        out_shape=jax.ShapeDtypeStruct((b, h, V_DIM, l), jnp.float32),
        grid=(b, h, nblk, nblk),
        in_specs=[pl.BlockSpec((None, None, t, FLASH_LANES), lambda bi, hi, qi, ki: (bi, hi, qi, 0)),
                  pl.BlockSpec((None, None, t, FLASH_LANES),
                               lambda bi, hi, qi, ki: (bi, hi, jnp.minimum(ki, qi), 0)),
                  pl.BlockSpec((None, None, V_DIM, t), lambda bi, hi, qi, ki: (bi, hi, 0, jnp.minimum(ki, qi)))],
        out_specs=pl.BlockSpec((None, None, V_DIM, t), lambda bi, hi, qi, ki: (bi, hi, 0, qi)),
        scratch_shapes=[pltpu.VMEM((1, t), jnp.float32), pltpu.VMEM((1, t), jnp.float32),
                        pltpu.VMEM((V_DIM, t), jnp.float32)],
        compiler_params=pltpu.CompilerParams(
            dimension_semantics=("arbitrary", "arbitrary", "arbitrary", "arbitrary"),
            vmem_limit_bytes=VMEM_LIMIT),
        name="prompt_attention",
    )(q, k, vt)


def mla_attend_prompt(qn, qr, kn, kr, v):
    b, l = qn.shape[:2]
    bf16 = jnp.bfloat16
    pad = jnp.zeros((b, l, MLA_HEADS, FLASH_LANES - NOPE_DIM - ROPE_DIM), bf16)
    q_cat = jnp.concatenate([qn.astype(bf16), qr.astype(bf16), pad], -1).transpose(0, 2, 1, 3)
    kr_h = jnp.broadcast_to(kr.astype(bf16)[:, :, None, :], (b, l, MLA_HEADS, ROPE_DIM))
    k_cat = jnp.concatenate([kn.astype(bf16), kr_h, pad], -1).transpose(0, 2, 1, 3)
    o = _flash_attention(q_cat, k_cat, v.astype(bf16).transpose(0, 2, 3, 1))
    return o.transpose(0, 3, 1, 2).reshape(b, l, MLA_HEADS * V_DIM)


DECODE_PAGES = 8
DECODE_NEW_ROWS = 8


def _decode_kernel(pt_ref, *refs):
    f32, bf16 = jnp.float32, jnp.bfloat16
    npg = DECODE_PAGES
    c_refs, kr_refs = refs[:npg], refs[npg:2 * npg]
    (qn_ref, qr_ref, cnew_ref, krnew_ref, wuk_ref, ind_ref, o_ref, m_scr, l_scr, acc_scr) = refs[2 * npg:]
    j = pl.program_id(1)
    nt = (((1,), (1,)), ((), ()))
    rows = qn_ref.shape[0]

    @pl.when(j == 0)
    def _():
        m_scr[...] = jnp.full(m_scr.shape, -jnp.inf, f32)
        l_scr[...] = jnp.zeros(l_scr.shape, f32)
        acc_scr[...] = jnp.zeros(acc_scr.shape, f32)

    def scores(c, kr):
        cb = c.astype(bf16)
        k = jnp.dot(cb, wuk_ref[...], preferred_element_type=f32)
        ms = lax.dot_general(ind_ref[...], (k * k).astype(bf16), nt, preferred_element_type=f32)
        r = lax.rsqrt(ms[:rows] * (1.0 / NOPE_DIM) + EPS)
        s = r * lax.dot_general(qn_ref[...], k.astype(bf16), nt, preferred_element_type=f32)
        s = s + lax.dot_general(qr_ref[...], kr.astype(bf16), nt, preferred_element_type=f32)
        return s * ATTN_SCALE, cb

    def update(s, cb):
        m_prev = m_scr[...]
        m_new = jnp.maximum(m_prev, jnp.max(s, axis=-1, keepdims=True))
        alpha = jnp.exp(m_prev - m_new)
        p = jnp.exp(s - m_new)
        l_scr[...] = alpha * l_scr[...] + jnp.sum(p, axis=-1, keepdims=True)
        acc_scr[...] = alpha * acc_scr[...] + jnp.dot(p.astype(bf16), cb, preferred_element_type=f32)
        m_scr[...] = m_new

    c_all = jnp.concatenate([r[...] for r in c_refs], 0)
    kr_all = jnp.concatenate([r[...] for r in kr_refs], 0)
    update(*scores(c_all, kr_all))

    @pl.when(j == pl.num_programs(1) - 1)
    def _():
        s, cb = scores(cnew_ref[...], krnew_ref[...])
        key = lax.broadcasted_iota(jnp.int32, s.shape, 1)
        query = lax.broadcasted_iota(jnp.int32, s.shape, 0) // MLA_HEADS
        update(jnp.where(key <= query, s, -jnp.inf), cb)
        o_ref[...] = acc_scr[...] / l_scr[...]


def mla_attend_sample(qn, qr, ckv_new, kr_new, cache_ckv, cache_krope, page_table, w_uk, g_kn, w_uv):
    b, lq = qn.shape[:2]
    f32, bf16 = jnp.float32, jnp.bfloat16
    n_pages = page_table.shape[1]
    npg = DECODE_PAGES
    assert n_pages % npg == 0 and lq <= DECODE_NEW_ROWS
    rows = lq * MLA_HEADS
    hd = MLA_HEADS * NOPE_DIM
    eye = jnp.eye(MLA_HEADS, dtype=f32)
    qg = qn.astype(f32) * g_kn.astype(f32)
    qn_bd = (qg[:, :, :, None, :] * eye[None, None, :, :, None]).reshape(b, rows, hd).astype(bf16)
    qr_rows = qr.reshape(b, rows, ROPE_DIM).astype(bf16)
    ind = jnp.tile(jnp.repeat(eye, NOPE_DIM, axis=1), (lq, 1))
    ind = jnp.pad(ind, ((0, -rows % 128), (0, 0))).astype(bf16)
    pad_new = ((0, 0), (0, DECODE_NEW_ROWS - lq), (0, 0))
    cnew = jnp.pad(ckv_new.astype(f32), pad_new)
    krnew = jnp.pad(kr_new.astype(f32), pad_new)

    def page_spec(width, i):
        return pl.BlockSpec((None, PAGE_SIZE, width), lambda bi, j, pt: (pt[bi, j * npg + i], 0, 0))

    def batch_spec(shape):
        return pl.BlockSpec((None,) + shape, lambda bi, j, pt: (bi, 0, 0))

    grid_spec = pltpu.PrefetchScalarGridSpec(
        num_scalar_prefetch=1,
        grid=(b, n_pages // npg),
        in_specs=([page_spec(KV_LORA, i) for i in range(npg)] + [page_spec(ROPE_DIM, i) for i in range(npg)] +
                  [batch_spec((rows, hd)), batch_spec((rows, ROPE_DIM)),
                   batch_spec((DECODE_NEW_ROWS, KV_LORA)), batch_spec((DECODE_NEW_ROWS, ROPE_DIM)),
                   pl.BlockSpec((KV_LORA, hd), lambda bi, j, pt: (0, 0)),
                   pl.BlockSpec(ind.shape, lambda bi, j, pt: (0, 0))]),
        out_specs=batch_spec((rows, KV_LORA)),
        scratch_shapes=[pltpu.VMEM((rows, 1), f32), pltpu.VMEM((rows, 1), f32), pltpu.VMEM((rows, KV_LORA), f32)],
    )
    o_lat = pl.pallas_call(
        _decode_kernel,
        grid_spec=grid_spec,
        out_shape=jax.ShapeDtypeStruct((b, rows, KV_LORA), f32),
        compiler_params=pltpu.CompilerParams(dimension_semantics=("arbitrary", "arbitrary"),
                                             vmem_limit_bytes=VMEM_LIMIT),
        name="sample_attention",
    )(page_table, *([cache_ckv] * npg), *([cache_krope] * npg), qn_bd, qr_rows, cnew, krnew,
      w_uk.astype(bf16), ind)
    o_lat = o_lat.reshape(b, lq, MLA_HEADS, KV_LORA)
    o = jnp.einsum('bqhc,chd->bqhd', o_lat, w_uv.astype(f32).reshape(KV_LORA, MLA_HEADS, V_DIM))
    return o.reshape(b, lq, MLA_HEADS * V_DIM).astype(qn.dtype)


PEER_ROUTE_TOKENS = 256
PEER_LANE_TILE = 128
PEER_GATHER_TOKENS = 8
PEER_ROW_TILE = (8, 128)
PEER_PAIRS = PEER_HEADS * PEER_TOPK
PEER_DMA_QUEUES = 2


def _topk_rows(s, k):
    n = s.shape[0]
    iota = lax.broadcasted_iota(jnp.int32, s.shape, 0).astype(jnp.float32)
    vals, idxs = [], []
    for _ in range(k):
        m = jnp.max(s, axis=0, keepdims=True)
        ix = jnp.min(jnp.where(s == m, iota, float(n)), axis=0, keepdims=True)
        vals.append(m)
        idxs.append(ix)
        s = jnp.where(iota == ix, -jnp.inf, s)
    return jnp.concatenate(vals, 0), jnp.concatenate(idxs, 0)


def _select_rows(table, sel):
    out = jnp.zeros(sel.shape, table.dtype)
    for j in range(table.shape[0]):
        out = jnp.where(sel == float(j), table[j:j + 1, :], out)
    return out


def _peer_route_kernel(x_ref, g_ref, wq_ref, gq_ref, k1_ref, k2_ref, xn_ref, idx_ref, gate_ref, xn_scr):
    f32, bf16 = jnp.float32, jnp.bfloat16

    @pl.when(pl.program_id(1) == 0)
    def _():
        x = x_ref[...]
        y = x * lax.rsqrt(jnp.mean(x * x, -1, keepdims=True) + EPS) * g_ref[...]
        xn_ref[...] = y
        xn_scr[...] = y.astype(bf16)

    q = jnp.dot(xn_scr[...], wq_ref[...], preferred_element_type=f32)
    qn = (q * lax.rsqrt(jnp.mean(q * q, -1, keepdims=True) + EPS) * gq_ref[...]).astype(bf16)
    half = PEER_QDIM // 2
    nt = (((1,), (1,)), ((), ()))
    s1 = lax.dot_general(k1_ref[...], qn[:, :half], nt, preferred_element_type=f32)
    s2 = lax.dot_general(k2_ref[...], qn[:, half:], nt, preferred_element_type=f32)
    for c in range(s1.shape[1] // PEER_LANE_TILE):
        cs = slice(c * PEER_LANE_TILE, (c + 1) * PEER_LANE_TILE)
        v1, i1 = _topk_rows(s1[:, cs], PEER_TOPK)
        v2, i2 = _topk_rows(s2[:, cs], PEER_TOPK)
        cand = jnp.concatenate([v1[a:a + 1, :] + v2 for a in range(PEER_TOPK)], 0)
        vals, ci = _topk_rows(cand, PEER_TOPK)
        a = jnp.floor(ci * (1.0 / PEER_TOPK))
        b = ci - a * PEER_TOPK
        e = _select_rows(i1, a) * float(PEER_KEYS) + _select_rows(i2, b)
        ex = jnp.exp(vals - vals[0:1, :])
        idx_ref[:, cs] = e.astype(jnp.int32)
        gate_ref[:, cs] = ex / jnp.sum(ex, axis=0, keepdims=True)


def _peer_route(x, g, w_q, g_q, sub_keys):
    t, d = x.shape
    tb = PEER_ROUTE_TOKENS
    assert t % tb == 0
    bf16 = jnp.bfloat16
    half = PEER_QDIM // 2
    return pl.pallas_call(
        _peer_route_kernel,
        out_shape=(jax.ShapeDtypeStruct((t, d), jnp.float32),
                   jax.ShapeDtypeStruct((PEER_PAIRS, t), jnp.int32),
                   jax.ShapeDtypeStruct((PEER_PAIRS, t), jnp.float32)),
        grid=(t // tb, PEER_HEADS),
        in_specs=[pl.BlockSpec((tb, d), lambda i, h: (i, 0)),
                  pl.BlockSpec((1, d), lambda i, h: (0, 0)),
                  pl.BlockSpec((d, PEER_QDIM), lambda i, h: (0, h)),
                  pl.BlockSpec((1, PEER_QDIM), lambda i, h: (0, 0)),
                  pl.BlockSpec((PEER_KEYS, half), lambda i, h: (0, 0)),
                  pl.BlockSpec((PEER_KEYS, half), lambda i, h: (0, 0))],
        out_specs=(pl.BlockSpec((tb, d), lambda i, h: (i, 0)),
                   pl.BlockSpec((PEER_TOPK, tb), lambda i, h: (h, i)),
                   pl.BlockSpec((PEER_TOPK, tb), lambda i, h: (h, i))),
        scratch_shapes=[pltpu.VMEM((tb, d), bf16)],
        compiler_params=pltpu.CompilerParams(dimension_semantics=("arbitrary", "arbitrary")),
        name="peer_route",
    )(x, g.reshape(1, d), w_q.astype(bf16), g_q.reshape(1, PEER_QDIM),
      sub_keys[0].astype(bf16), sub_keys[1].astype(bf16))


def _peer_expert_kernel(idx_ref, idx_next_ref, x_ref, gate_ref, res_ref, diag_ref, fold_ref, spread_ref, uv_hbm,
                        out_ref, buf, sems):
    f32, bf16 = jnp.float32, jnp.bfloat16
    tb = PEER_GATHER_TOKENS
    sub = PEER_ROW_TILE[0]
    rows = tb * PEER_PAIRS
    i = pl.program_id(0)
    n = pl.num_programs(0)
    slot = i % 2

    def start_gather(ids_ref, dst_slot):
        def per_token(t, carry):
            base = dst_slot * rows + t * PEER_PAIRS
            for k in range(PEER_PAIRS):
                pltpu.make_async_copy(uv_hbm.at[ids_ref[t, k]], buf.at[base + k], sems.at[dst_slot]).start(
                    priority=k % PEER_DMA_QUEUES)
            return carry
        lax.fori_loop(0, tb, per_token, 0)

    @pl.when(i == 0)
    def _():
        start_gather(idx_ref, 0)

    @pl.when(i + 1 < n)
    def _():
        start_gather(idx_next_ref, 1 - slot)

    pltpu.make_async_copy(uv_hbm.at[pl.ds(0, rows)], buf.at[pl.ds(slot * rows, rows)], sems.at[slot]).wait()

    nt = (((1,), (1,)), ((), ()))
    diag = diag_ref[...]
    flat = (PEER_PAIRS * sub, PEER_ROW_TILE[1])

    def slab_rows(t, first):
        tok = pl.ds(slot * rows + t * PEER_PAIRS, PEER_PAIRS)
        return buf[tok, first:first + sub, :].reshape(flat).astype(bf16)

    e = jnp.concatenate([lax.dot_general(x_ref[t].astype(bf16), slab_rows(t, 0), nt, preferred_element_type=f32)
                         * diag for t in range(tb)], 0)
    e_hi = e.astype(bf16)
    e_lo = (e - e_hi.astype(f32)).astype(bf16)
    h = jnp.dot(jnp.concatenate([e_hi, e_lo], 0), fold_ref[...], preferred_element_type=f32)
    h = h[:tb * sub] + h[tb * sub:]
    h = jnp.sum(h.reshape(tb, sub, PEER_PAIRS), axis=1)
    act = 0.5 * h * (1.0 + lax.erf(h * math.sqrt(0.5)))
    w = (gate_ref[...] * act).astype(bf16)
    wrep = jnp.broadcast_to(w[:, None, :], (tb, sub, PEER_PAIRS)).reshape(tb * sub, PEER_PAIRS)
    wfull = jnp.dot(wrep, spread_ref[...], preferred_element_type=f32)
    for t in range(tb):
        wm = (wfull[t * sub:(t + 1) * sub] * diag).astype(bf16)
        out_ref[t] = res_ref[t] + jnp.dot(wm, slab_rows(t, sub), preferred_element_type=f32)


def _peer_experts(xn, res, idx, gates, u_tab, v_tab):
    t, d = xn.shape
    tb = PEER_GATHER_TOKENS
    sub, lanes = PEER_ROW_TILE
    assert t % tb == 0 and d == sub * lanes
    nblk = t // tb
    rows = tb * PEER_PAIRS
    n_exp = u_tab.shape[0]
    uv = jnp.concatenate([u_tab.reshape(n_exp, sub, lanes), v_tab.reshape(n_exp, sub, lanes)], axis=1)
    lane = jnp.arange(d)
    diag = (lane[None, :] % sub == jnp.arange(sub)[:, None]).astype(jnp.float32)
    fold = (lane[:, None] // sub == jnp.arange(PEER_PAIRS)[None, :]).astype(jnp.bfloat16)
    tile_spec = pl.BlockSpec((tb, sub, lanes), lambda i: (i, 0, 0))
    full_spec = lambda a: pl.BlockSpec(a.shape, lambda i: (0, 0))
    out = pl.pallas_call(
        _peer_expert_kernel,
        out_shape=jax.ShapeDtypeStruct((t, sub, lanes), jnp.float32),
        grid=(nblk,),
        in_specs=[pl.BlockSpec((tb, PEER_PAIRS), lambda i: (i, 0), memory_space=pltpu.SMEM),
                  pl.BlockSpec((tb, PEER_PAIRS), lambda i: (jnp.minimum(i + 1, nblk - 1), 0),
                               memory_space=pltpu.SMEM),
                  tile_spec,
                  pl.BlockSpec((tb, PEER_PAIRS), lambda i: (i, 0)),
                  tile_spec,
                  full_spec(diag), full_spec(fold), full_spec(fold.T),
                  pl.BlockSpec(memory_space=pl.ANY)],
        out_specs=tile_spec,
        scratch_shapes=[pltpu.VMEM((2 * rows, 2 * sub, lanes), jnp.float32),
                        pltpu.SemaphoreType.DMA((2,))],
        compiler_params=pltpu.CompilerParams(dimension_semantics=("arbitrary",), vmem_limit_bytes=VMEM_LIMIT),
        name="peer_experts",
    )(idx, idx, xn.reshape(t, sub, lanes), gates, res.reshape(t, sub, lanes), diag, fold, fold.T, uv)
    return out.reshape(t, d)


def peer(h, g, w_q, g_q, sub_keys, u_tab, v_tab):
    xn, idx_t, gates_t = _peer_route(h, g, w_q, g_q, sub_keys)
    return _peer_experts(xn, h, idx_t.T, gates_t.T, u_tab, v_tab)


def kernel(x_prompt, x_sample, p_prompt, p_sample, state_ssm, state_conv, cache_ckv, cache_krope, page_table,
           ln_mix, ln_ffn, ln_ple,
           m_w_in, m_conv_w, m_conv_b, m_dt_bias, m_a_log, m_d, m_g_norm, m_w_out,
           kv_g_in, kv_w_dkv, kv_g_ckv, kv_w_kr, kv_g_kr, kv_w_uk, kv_g_kn, kv_w_uv,
           q_w_dq, q_g_cq, q_w_uq, q_g_qn, q_g_qr, a_w_o,
           peer_w_q, peer_g_q, peer_sub_keys, peer_u, peer_v,
           ple_w_proj, ple_w_gate):
    bf16 = jnp.bfloat16
    bp, lp = x_prompt.shape[:2]
    bs, ls = x_sample.shape[:2]
    n_p, n_s = bp * lp, bs * ls
    pos_p = jnp.arange(lp)
    pos_s = page_table.shape[1] * PAGE_SIZE + jnp.arange(ls)
    h = jnp.concatenate([x_prompt.reshape(n_p, D_MODEL), x_sample.reshape(n_s, D_MODEL)], 0)
    p_all = jnp.concatenate([p_prompt.reshape(DEPTH, n_p, PLE_DIM), p_sample.reshape(DEPTH, n_s, PLE_DIM)], 1)

    def split(a, tail):
        return a[:n_p].reshape((bp, lp) + tail), a[n_p:].reshape((bs, ls) + tail)

    ssm_p, conv_p, ssm_s, conv_s = [], [], [], []
    for i in range(DEPTH):
        if i < N_A:
            w_in = m_w_in[i].astype(bf16)
            z, xbc, dt = _dense(h, [w_in[:, :D_INNER], w_in[:, D_INNER:D_INNER + CONV_DIM],
                                    w_in[:, D_INNER + CONV_DIM:]], gain=ln_mix[i])
            mw = (m_conv_w[i], m_conv_b[i], m_dt_bias[i], m_a_log[i], m_d[i], m_g_norm[i])
            (z_p, z_s), (xbc_p, xbc_s), (dt_p, dt_s) = (split(z, (D_INNER,)), split(xbc, (CONV_DIM,)),
                                                        split(dt, (SSM_HEADS,)))
            conv0 = jnp.zeros((bp, CONV_W - 1, CONV_DIM), h.dtype)
            ssm0 = jnp.zeros((bp, SSM_HEADS, SSM_HEADDIM, SSM_STATE), h.dtype)
            y_p, c_p, s_p = mamba_sequence(z_p, xbc_p, dt_p, conv0, ssm0, *mw)
            y_s, c_s, s_s = mamba_sequence(z_s, xbc_s, dt_s, state_conv[i], state_ssm[i], *mw)
            conv_p.append(c_p)
            ssm_p.append(s_p)
            conv_s.append(c_s)
            ssm_s.append(s_s)
            y = jnp.concatenate([y_p.reshape(n_p, D_INNER), y_s.reshape(n_s, D_INNER)], 0)
            (h,) = _dense(y, [m_w_out[i].astype(bf16)], residual=h)
        else:
            if i == N_A:
                ckv_raw, kr_raw = _dense(h, [kv_w_dkv.astype(bf16), kv_w_kr.astype(bf16)], gain=kv_g_in)
                ckv = rms_norm(ckv_raw, kv_g_ckv)
                ckv_p, ckv_s = split(ckv, (KV_LORA,))
                kr_p, kr_s = split(rms_norm(kr_raw, kv_g_kr), (ROPE_DIM,))
                kr_p, kr_s = rope(kr_p, pos_p), rope(kr_s, pos_s)
                k_raw, v_p = _dense(ckv[:n_p], [kv_w_uk.astype(bf16), kv_w_uv.astype(bf16)])
                kn_p = rms_norm(k_raw.reshape(bp, lp, MLA_HEADS, NOPE_DIM), kv_g_kn)
                v_p = v_p.reshape(bp, lp, MLA_HEADS, V_DIM)
            j = i - N_A
            (cq_raw,) = _dense(h, [q_w_dq[j].astype(bf16)], gain=ln_mix[i])
            (q,) = _dense(cq_raw, [q_w_uq[j].astype(bf16)], gain=q_g_cq[j])
            q = q.reshape(n_p + n_s, MLA_HEADS, NOPE_DIM + ROPE_DIM)
            qn_p, qn_s = split(rms_norm(q[..., :NOPE_DIM], q_g_qn[j]), (MLA_HEADS, NOPE_DIM))
            qr_p, qr_s = split(rms_norm(q[..., NOPE_DIM:], q_g_qr[j]), (MLA_HEADS, ROPE_DIM))
            qr_p, qr_s = rope(qr_p, pos_p), rope(qr_s, pos_s)
            o_p = mla_attend_prompt(qn_p, qr_p, kn_p, kr_p, v_p)
            o_s = mla_attend_sample(qn_s, qr_s, ckv_s, kr_s, cache_ckv, cache_krope, page_table,
                                    kv_w_uk, kv_g_kn, kv_w_uv)
            o = jnp.concatenate([o_p.reshape(n_p, MLA_HEADS * V_DIM), o_s.reshape(n_s, MLA_HEADS * V_DIM)], 0)
            (h,) = _dense(o, [a_w_o[j].astype(bf16)], residual=h)
        h = peer(h, ln_ffn[i], peer_w_q[i], peer_g_q[i], peer_sub_keys[i], peer_u[i], peer_v[i])
        h = _ple(h, p_all[i], ln_ple[i], ple_w_proj[i], ple_w_gate[i])
    hp, hs = split(h, (D_MODEL,))
    return (hp, hs, jnp.stack(ssm_p), jnp.stack(conv_p), ckv_p, kr_p,
            jnp.stack(ssm_s), jnp.stack(conv_s), ckv_s, kr_s)
```

```python
import functools
import math

import jax
import jax.numpy as jnp
from jax import lax
from jax.experimental import pallas as pl
from jax.experimental.pallas import tpu as pltpu

D_MODEL = 1024
DEPTH = 2
PAGE_SIZE = 128
N_A = DEPTH // 2
D_INNER = 2 * D_MODEL
SSM_HEADDIM = 64
SSM_HEADS = D_INNER // SSM_HEADDIM
SSM_GROUPS = 4
SSM_STATE = 128
CONV_W = 4
CONV_DIM = D_INNER + 2 * SSM_GROUPS * SSM_STATE
SSD_CHUNK = 128
MLA_HEADS = 16
Q_LORA = 384
KV_LORA = 256
NOPE_DIM = 64
ROPE_DIM = 32
V_DIM = 64
ROPE_THETA = 10000.0
ATTN_SCALE = (NOPE_DIM + ROPE_DIM) ** -0.5
PEER_HEADS = 8
PEER_KEYS = 128
PEER_QDIM = 256
PEER_TOPK = 16
PLE_DIM = 256
EPS = 1e-6

VMEM_LIMIT = 48 * 1024 * 1024


def rms_norm(x, g):
    xf = x.astype(jnp.float32)
    y = xf * lax.rsqrt(jnp.mean(xf * xf, -1, keepdims=True) + EPS)
    return (y * g.astype(jnp.float32)).astype(x.dtype)


def rope(x, pos):
    half = ROPE_DIM // 2
    inv = ROPE_THETA ** (-jnp.arange(half, dtype=jnp.float32) / half)
    ang = pos.astype(jnp.float32)[:, None] * inv
    ang = ang.reshape((1, ang.shape[0]) + (1,) * (x.ndim - 3) + (half,))
    cos, sin = jnp.cos(ang), jnp.sin(ang)
    xf = x.astype(jnp.float32)
    x1, x2 = xf[..., :half], xf[..., half:]
    return jnp.concatenate([x1 * cos - x2 * sin, x1 * sin + x2 * cos], -1).astype(x.dtype)


DENSE_ROWS = 256


def _dense_kernel(*refs, n_out, has_gain, has_residual):
    refs = list(refs)
    x_ref = refs.pop(0)
    g_ref = refs.pop(0) if has_gain else None
    w_refs = [refs.pop(0) for _ in range(n_out)]
    r_ref = refs.pop(0) if has_residual else None
    x = x_ref[...]
    if has_gain:
        x = x * lax.rsqrt(jnp.mean(x * x, -1, keepdims=True) + EPS) * g_ref[...]
    xb = x.astype(jnp.bfloat16)
    for w_ref, o_ref in zip(w_refs, refs):
        y = jnp.dot(xb, w_ref[...], preferred_element_type=jnp.float32)
        o_ref[...] = y + r_ref[...] if has_residual else y


def _dense(x, weights, gain=None, residual=None):
    rows, k = x.shape
    tm = math.gcd(rows, DENSE_ROWS)
    assert residual is None or len(weights) == 1
    row_spec = lambda n: pl.BlockSpec((tm, n), lambda i: (i, 0))
    full_spec = lambda a: pl.BlockSpec(a.shape, lambda i: (0, 0))
    args, specs = [x], [row_spec(k)]
    if gain is not None:
        args.append(gain.reshape(1, k).astype(jnp.float32))
        specs.append(full_spec(args[-1]))
    for w in weights:
        args.append(w)
        specs.append(full_spec(w))
    if residual is not None:
        args.append(residual)
        specs.append(row_spec(residual.shape[1]))
    return pl.pallas_call(
        functools.partial(_dense_kernel, n_out=len(weights), has_gain=gain is not None,
                          has_residual=residual is not None),
        out_shape=tuple(jax.ShapeDtypeStruct((rows, w.shape[1]), jnp.float32) for w in weights),
        grid=(rows // tm,),
        in_specs=specs,
        out_specs=tuple(row_spec(w.shape[1]) for w in weights),
        compiler_params=pltpu.CompilerParams(dimension_semantics=("arbitrary",), vmem_limit_bytes=VMEM_LIMIT),
        name="dense",
    )(*args)


def _ple_kernel(h_ref, p_ref, g_ref, wp_ref, wg_ref, o_ref):
    f32, bf16 = jnp.float32, jnp.bfloat16
    h = h_ref[...]
    hn = h * lax.rsqrt(jnp.mean(h * h, -1, keepdims=True) + EPS) * g_ref[...]
    gate = jax.nn.sigmoid(jnp.dot(hn.astype(bf16), wg_ref[...], preferred_element_type=f32))
    proj = jnp.dot(p_ref[...].astype(bf16), wp_ref[...], preferred_element_type=f32)
    o_ref[...] = h + proj * gate


def _ple(h, p, g, w_proj, w_gate):
    rows, d = h.shape
    tm = math.gcd(rows, DENSE_ROWS)
    row_spec = lambda n: pl.BlockSpec((tm, n), lambda i: (i, 0))
    full_spec = lambda a: pl.BlockSpec(a.shape, lambda i: (0, 0))
    g2 = g.reshape(1, d).astype(jnp.float32)
    wp, wg = w_proj.astype(jnp.bfloat16), w_gate.astype(jnp.bfloat16)
    return pl.pallas_call(
        _ple_kernel,
        out_shape=jax.ShapeDtypeStruct((rows, d), jnp.float32),
        grid=(rows // tm,),
        in_specs=[row_spec(d), row_spec(p.shape[1]), full_spec(g2), full_spec(wp), full_spec(wg)],
        out_specs=row_spec(d),
        compiler_params=pltpu.CompilerParams(dimension_semantics=("arbitrary",), vmem_limit_bytes=VMEM_LIMIT),
        name="ple",
    )(h, p, g2, wp, wg)


SSD_TAIL = 8


def _ssd_kernel(z_ref, xbc_ref, dt_ref, prev_ref, h0_ref, cw_ref, cb_ref, dtb_ref, a_ref, dskip_ref, gn_ref,
                expand_ref, expand_t_ref, y_ref, hout_ref, ext_scr, h_scr, *, n_valid):
    f32, bf16 = jnp.float32, jnp.bfloat16
    hi = lax.Precision.HIGHEST
    q = z_ref.shape[0]
    c = pl.program_id(1)
    heads_per_group = SSM_HEADS // SSM_GROUPS
    gw = heads_per_group * SSM_HEADDIM
    nt = (((1,), (1,)), ((), ()))
    tn = (((0,), (0,)), ((), ()))

    @pl.when(c == 0)
    def _():
        ext_scr[0:SSD_TAIL, :] = prev_ref[...]
        h_scr[...] = h0_ref[...]

    ext_scr[SSD_TAIL:SSD_TAIL + q, :] = xbc_ref[...]
    conv = cb_ref[...]
    for w in range(CONV_W):
        conv = conv + ext_scr[pl.ds(SSD_TAIL - (CONV_W - 1) + w, q), :] * cw_ref[w:w + 1, :]
    tail = ext_scr[q:q + SSD_TAIL, :]
    ext_scr[0:SSD_TAIL, :] = tail
    act = conv * jax.nn.sigmoid(conv)
    xs = act[:, :D_INNER]
    bmat = act[:, D_INNER:D_INNER + SSM_GROUPS * SSM_STATE].astype(bf16)
    cmat = act[:, D_INNER + SSM_GROUPS * SSM_STATE:].astype(bf16)

    row = lax.broadcasted_iota(jnp.int32, (q, q), 0)
    col = lax.broadcasted_iota(jnp.int32, (q, q), 1)
    dt = jax.nn.softplus(dt_ref[...] + dtb_ref[...])
    if n_valid < q:
        dt = jnp.where(lax.broadcasted_iota(jnp.int32, dt.shape, 0) < n_valid, dt, 0.0)
    a = dt * a_ref[...]
    a_cum = jnp.dot((col <= row).astype(f32), a, precision=hi, preferred_element_type=f32)
    a_cum_t = jnp.dot(a.T, (row <= col).astype(f32), precision=hi, preferred_element_type=f32)
    a_last = a_cum[q - 1:q, :]
    expand = expand_ref[...]
    widen = lambda v: jnp.dot(v, expand, precision=hi, preferred_element_type=f32)
    xd = xs * widen(dt)
    xd_b = xd.astype(bf16)
    xdd_b = (xd * widen(jnp.exp(a_last - a_cum))).astype(bf16)
    grow = widen(jnp.exp(a_cum))
    carry = jnp.dot(expand_t_ref[...], jnp.exp(a_cum_t[:, q - 1:q]), precision=hi,
                    preferred_element_type=f32)

    y_parts = []
    for g in range(SSM_GROUPS):
        bg = bmat[:, g * SSM_STATE:(g + 1) * SSM_STATE]
        cg = cmat[:, g * SSM_STATE:(g + 1) * SSM_STATE]
        cbg = lax.dot_general(cg, bg, nt, preferred_element_type=f32)
        h_in = h_scr[g * gw:(g + 1) * gw, :]
        y_off = lax.dot_general(cg, h_in.astype(bf16), nt, preferred_element_type=f32)
        y_g = y_off * grow[:, g * gw:(g + 1) * gw]
        diag_parts = []
        for r in range(heads_per_group):
            hd = g * heads_per_group + r
            seg = a_cum[:, hd:hd + 1] - a_cum_t[hd:hd + 1, :]
            lmat = jnp.exp(jnp.where(col <= row, seg, -jnp.inf))
            lanes = slice(hd * SSM_HEADDIM, (hd + 1) * SSM_HEADDIM)
            diag_parts.append(jnp.dot((cbg * lmat).astype(bf16), xd_b[:, lanes], preferred_element_type=f32))
        y_parts.append(y_g + jnp.concatenate(diag_parts, 1))
        st = lax.dot_general(xdd_b[:, g * gw:(g + 1) * gw], bg, tn, preferred_element_type=f32)
        h_scr[g * gw:(g + 1) * gw, :] = h_in * carry[g * gw:(g + 1) * gw, :] + st
    y = jnp.concatenate(y_parts, 1) + dskip_ref[...] * xs
    zz = z_ref[...]
    y = y * (zz * jax.nn.sigmoid(zz))
    gsz = D_INNER // SSM_GROUPS
    outs = []
    for g in range(SSM_GROUPS):
        seg = y[:, g * gsz:(g + 1) * gsz]
        outs.append(seg * lax.rsqrt(jnp.mean(seg * seg, -1, keepdims=True) + EPS))
    y_ref[...] = jnp.concatenate(outs, 1) * gn_ref[...]

    @pl.when(c == pl.num_programs(1) - 1)
    def _():
        hout_ref[...] = h_scr[...]


def _ssd(z, xbc, dt, conv_prev, h0, conv_w, conv_b, dt_bias, a_log, d_skip, g_norm, chunk, n_valid):
    b, l, _ = z.shape
    f32 = jnp.float32
    assert l % chunk == 0
    prev = jnp.pad(conv_prev.astype(f32), ((0, 0), (SSD_TAIL - (CONV_W - 1), 0), (0, 0)))
    rows = SSM_HEADS * SSM_HEADDIM
    expand = jnp.repeat(jnp.eye(SSM_HEADS, dtype=f32), SSM_HEADDIM, axis=1)
    vec = lambda v, n: v.astype(f32).reshape(1, n)
    seq_spec = lambda n: pl.BlockSpec((None, chunk, n), lambda bi, ci: (bi, ci, 0))
    bat_spec = lambda r, n: pl.BlockSpec((None, r, n), lambda bi, ci: (bi, 0, 0))
    full_spec = lambda a: pl.BlockSpec(a.shape, lambda bi, ci: (0, 0))
    params = [conv_w.astype(f32), vec(conv_b, CONV_DIM), vec(dt_bias, SSM_HEADS), vec(-jnp.exp(a_log.astype(f32)), SSM_HEADS),
              vec(jnp.repeat(d_skip, SSM_HEADDIM), D_INNER), vec(g_norm, D_INNER), expand, expand.T]
    y, h_new = pl.pallas_call(
        functools.partial(_ssd_kernel, n_valid=n_valid),
        out_shape=(jax.ShapeDtypeStruct((b, l, D_INNER), f32), jax.ShapeDtypeStruct((b, rows, SSM_STATE), f32)),
        grid=(b, l // chunk),
        in_specs=[seq_spec(D_INNER), seq_spec(CONV_DIM), seq_spec(SSM_HEADS), bat_spec(SSD_TAIL, CONV_DIM),
                  bat_spec(rows, SSM_STATE)] + [full_spec(p) for p in params],
        out_specs=(seq_spec(D_INNER), bat_spec(rows, SSM_STATE)),
        scratch_shapes=[pltpu.VMEM((chunk + SSD_TAIL, CONV_DIM), f32), pltpu.VMEM((rows, SSM_STATE), f32)],
        compiler_params=pltpu.CompilerParams(dimension_semantics=("arbitrary", "arbitrary"),
                                             vmem_limit_bytes=VMEM_LIMIT),
        name="ssd",
    )(z, xbc, dt, prev, h0.astype(f32).reshape(b, rows, SSM_STATE), *params)
    return y, h_new.reshape(b, SSM_HEADS, SSM_HEADDIM, SSM_STATE)


SSD_PAD_ROWS = 16


def mamba_sequence(z, xbc, dt, conv_prev, h0, conv_w, conv_b, dt_bias, a_log, d_skip, g_norm):
    b, l, _ = z.shape
    conv_new = jnp.concatenate([conv_prev.astype(xbc.dtype), xbc], 1)[:, -(CONV_W - 1):]
    mw = (conv_w, conv_b, dt_bias, a_log, d_skip, g_norm)
    if l % SSD_CHUNK == 0:
        y, h_new = _ssd(z, xbc, dt, conv_prev, h0, *mw, chunk=SSD_CHUNK, n_valid=SSD_CHUNK)
    else:
        assert l <= SSD_PAD_ROWS
        pad = lambda a: jnp.pad(a, ((0, 0), (0, SSD_PAD_ROWS - l), (0, 0)))
        y, h_new = _ssd(pad(z), pad(xbc), pad(dt), conv_prev, h0, *mw, chunk=SSD_PAD_ROWS, n_valid=l)
        y = y[:, :l]
    return y, conv_new, h_new.astype(z.dtype)


FLASH_TILE = 1024
FLASH_LANES = 128


def _flash_kernel(q_ref, k_ref, vt_ref, o_ref, m_scr, l_scr, acc_scr):
    f32 = jnp.float32
    qi, ki = pl.program_id(2), pl.program_id(3)

    @pl.when(ki == 0)
    def _():
        m_scr[...] = jnp.full(m_scr.shape, -jnp.inf, f32)
        l_scr[...] = jnp.zeros(l_scr.shape, f32)
        acc_scr[...] = jnp.zeros(acc_scr.shape, f32)

    def update(on_diagonal):
        s = lax.dot_general(k_ref[...], q_ref[...], (((1,), (1,)), ((), ())), preferred_element_type=f32)
        s = s * ATTN_SCALE
        if on_diagonal:
            key = lax.broadcasted_iota(jnp.int32, s.shape, 0)
            query = lax.broadcasted_iota(jnp.int32, s.shape, 1)
            s = jnp.where(key <= query, s, -jnp.inf)
        m_prev = m_scr[...]
        m_new = jnp.maximum(m_prev, jnp.max(s, axis=0, keepdims=True))
        alpha = jnp.exp(m_prev - m_new)
        p = jnp.exp(s - m_new)
        l_scr[...] = alpha * l_scr[...] + jnp.sum(p, axis=0, keepdims=True)
        acc_scr[...] = alpha * acc_scr[...] + jnp.dot(vt_ref[...], p.astype(vt_ref.dtype),
                                                      preferred_element_type=f32)
        m_scr[...] = m_new

    @pl.when(ki < qi)
    def _():
        update(False)

    @pl.when(ki == qi)
    def _():
        update(True)
        o_ref[...] = acc_scr[...] / l_scr[...]


def _flash_attention(q, k, vt):
    b, h, l, _ = q.shape
    t = min(FLASH_TILE, l)
    assert l % t == 0
    nblk = l // t
    return pl.pallas_call(
        _flash_kernel,
        out_shape=jax.ShapeDtypeStruct((b, h, V_DIM, l), jnp.float32),
        grid=(b, h, nblk, nblk),
        in_specs=[pl.BlockSpec((None, None, t, FLASH_LANES), lambda bi, hi, qi, ki: (bi, hi, qi, 0)),
                  pl.BlockSpec((None, None, t, FLASH_LANES),
                               lambda bi, hi, qi, ki: (bi, hi, jnp.minimum(ki, qi), 0)),
                  pl.BlockSpec((None, None, V_DIM, t), lambda bi, hi, qi, ki: (bi, hi, 0, jnp.minimum(ki, qi)))],
        out_specs=pl.BlockSpec((None, None, V_DIM, t), lambda bi, hi, qi, ki: (bi, hi, 0, qi)),
        scratch_shapes=[pltpu.VMEM((1, t), jnp.float32), pltpu.VMEM((1, t), jnp.float32),
                        pltpu.VMEM((V_DIM, t), jnp.float32)],
        compiler_params=pltpu.CompilerParams(
            dimension_semantics=("arbitrary", "arbitrary", "arbitrary", "arbitrary"),
            vmem_limit_bytes=VMEM_LIMIT),
        name="prompt_attention",
    )(q, k, vt)


def mla_attend_prompt(qn, qr, kn, kr, v):
    b, l = qn.shape[:2]
    bf16 = jnp.bfloat16
    pad = jnp.zeros((b, l, MLA_HEADS, FLASH_LANES - NOPE_DIM - ROPE_DIM), bf16)
    q_cat = jnp.concatenate([qn.astype(bf16), qr.astype(bf16), pad], -1).transpose(0, 2, 1, 3)
    kr_h = jnp.broadcast_to(kr.astype(bf16)[:, :, None, :], (b, l, MLA_HEADS, ROPE_DIM))
    k_cat = jnp.concatenate([kn.astype(bf16), kr_h, pad], -1).transpose(0, 2, 1, 3)
    o = _flash_attention(q_cat, k_cat, v.astype(bf16).transpose(0, 2, 3, 1))
    return o.transpose(0, 3, 1, 2).reshape(b, l, MLA_HEADS * V_DIM)


DECODE_PAGES = 8
DECODE_NEW_ROWS = 8


def _decode_kernel(pt_ref, *refs):
    f32, bf16 = jnp.float32, jnp.bfloat16
    npg = DECODE_PAGES
    c_refs, kr_refs = refs[:npg], refs[npg:2 * npg]
    (qn_ref, qr_ref, cnew_ref, krnew_ref, wuk_ref, ind_ref, o_ref, m_scr, l_scr, acc_scr) = refs[2 * npg:]
    j = pl.program_id(1)
    nt = (((1,), (1,)), ((), ()))
    rows = qn_ref.shape[0]

    @pl.when(j == 0)
    def _():
        m_scr[...] = jnp.full(m_scr.shape, -jnp.inf, f32)
        l_scr[...] = jnp.zeros(l_scr.shape, f32)
        acc_scr[...] = jnp.zeros(acc_scr.shape, f32)

    def scores(c, kr):
        cb = c.astype(bf16)
        k = jnp.dot(cb, wuk_ref[...], preferred_element_type=f32)
        ms = lax.dot_general(ind_ref[...], (k * k).astype(bf16), nt, preferred_element_type=f32)
        r = lax.rsqrt(ms[:rows] * (1.0 / NOPE_DIM) + EPS)
        s = r * lax.dot_general(qn_ref[...], k.astype(bf16), nt, preferred_element_type=f32)
        s = s + lax.dot_general(qr_ref[...], kr.astype(bf16), nt, preferred_element_type=f32)
        return s * ATTN_SCALE, cb

    def update(s, cb):
        m_prev = m_scr[...]
        m_new = jnp.maximum(m_prev, jnp.max(s, axis=-1, keepdims=True))
        alpha = jnp.exp(m_prev - m_new)
        p = jnp.exp(s - m_new)
        l_scr[...] = alpha * l_scr[...] + jnp.sum(p, axis=-1, keepdims=True)
        acc_scr[...] = alpha * acc_scr[...] + jnp.dot(p.astype(bf16), cb, preferred_element_type=f32)
        m_scr[...] = m_new

    c_all = jnp.concatenate([r[...] for r in c_refs], 0)
    kr_all = jnp.concatenate([r[...] for r in kr_refs], 0)
    update(*scores(c_all, kr_all))

    @pl.when(j == pl.num_programs(1) - 1)
    def _():
        s, cb = scores(cnew_ref[...], krnew_ref[...])
        key = lax.broadcasted_iota(jnp.int32, s.shape, 1)
        query = lax.broadcasted_iota(jnp.int32, s.shape, 0) // MLA_HEADS
        update(jnp.where(key <= query, s, -jnp.inf), cb)
        o_ref[...] = acc_scr[...] / l_scr[...]


def mla_attend_sample(qn, qr, ckv_new, kr_new, cache_ckv, cache_krope, page_table, w_uk, g_kn, w_uv):
    b, lq = qn.shape[:2]
    f32, bf16 = jnp.float32, jnp.bfloat16
    n_pages = page_table.shape[1]
    npg = DECODE_PAGES
    assert n_pages % npg == 0 and lq <= DECODE_NEW_ROWS
    rows = lq * MLA_HEADS
    hd = MLA_HEADS * NOPE_DIM
    eye = jnp.eye(MLA_HEADS, dtype=f32)
    qg = qn.astype(f32) * g_kn.astype(f32)
    qn_bd = (qg[:, :, :, None, :] * eye[None, None, :, :, None]).reshape(b, rows, hd).astype(bf16)
    qr_rows = qr.reshape(b, rows, ROPE_DIM).astype(bf16)
    ind = jnp.tile(jnp.repeat(eye, NOPE_DIM, axis=1), (lq, 1))
    ind = jnp.pad(ind, ((0, -rows % 128), (0, 0))).astype(bf16)
    pad_new = ((0, 0), (0, DECODE_NEW_ROWS - lq), (0, 0))
    cnew = jnp.pad(ckv_new.astype(f32), pad_new)
    krnew = jnp.pad(kr_new.astype(f32), pad_new)

    def page_spec(width, i):
        return pl.BlockSpec((None, PAGE_SIZE, width), lambda bi, j, pt: (pt[bi, j * npg + i], 0, 0))

    def batch_spec(shape):
        return pl.BlockSpec((None,) + shape, lambda bi, j, pt: (bi, 0, 0))

    grid_spec = pltpu.PrefetchScalarGridSpec(
        num_scalar_prefetch=1,
        grid=(b, n_pages // npg),
        in_specs=([page_spec(KV_LORA, i) for i in range(npg)] + [page_spec(ROPE_DIM, i) for i in range(npg)] +
                  [batch_spec((rows, hd)), batch_spec((rows, ROPE_DIM)),
                   batch_spec((DECODE_NEW_ROWS, KV_LORA)), batch_spec((DECODE_NEW_ROWS, ROPE_DIM)),
                   pl.BlockSpec((KV_LORA, hd), lambda bi, j, pt: (0, 0)),
                   pl.BlockSpec(ind.shape, lambda bi, j, pt: (0, 0))]),
        out_specs=batch_spec((rows, KV_LORA)),
        scratch_shapes=[pltpu.VMEM((rows, 1), f32), pltpu.VMEM((rows, 1), f32), pltpu.VMEM((rows, KV_LORA), f32)],
    )
    o_lat = pl.pallas_call(
        _decode_kernel,
        grid_spec=grid_spec,
        out_shape=jax.ShapeDtypeStruct((b, rows, KV_LORA), f32),
        compiler_params=pltpu.CompilerParams(dimension_semantics=("arbitrary", "arbitrary"),
                                             vmem_limit_bytes=VMEM_LIMIT),
        name="sample_attention",
    )(page_table, *([cache_ckv] * npg), *([cache_krope] * npg), qn_bd, qr_rows, cnew, krnew,
      w_uk.astype(bf16), ind)
    o_lat = o_lat.reshape(b, lq, MLA_HEADS, KV_LORA)
    o = jnp.einsum('bqhc,chd->bqhd', o_lat, w_uv.astype(f32).reshape(KV_LORA, MLA_HEADS, V_DIM))
    return o.reshape(b, lq, MLA_HEADS * V_DIM).astype(qn.dtype)


PEER_ROUTE_TOKENS = 256
PEER_LANE_TILE = 128
PEER_GATHER_TOKENS = 8
PEER_ROW_TILE = (8, 128)
PEER_PAIRS = PEER_HEADS * PEER_TOPK
PEER_DMA_QUEUES = 2


def _topk_rows(s, k):
    n = s.shape[0]
    iota = lax.broadcasted_iota(jnp.int32, s.shape, 0).astype(jnp.float32)
    vals, idxs = [], []
    for _ in range(k):
        m = jnp.max(s, axis=0, keepdims=True)
        ix = jnp.min(jnp.where(s == m, iota, float(n)), axis=0, keepdims=True)
        vals.append(m)
        idxs.append(ix)
        s = jnp.where(iota == ix, -jnp.inf, s)
    return jnp.concatenate(vals, 0), jnp.concatenate(idxs, 0)


def _select_rows(table, sel):
    out = jnp.zeros(sel.shape, table.dtype)
    for j in range(table.shape[0]):
        out = jnp.where(sel == float(j), table[j:j + 1, :], out)
    return out


def _peer_route_kernel(x_ref, g_ref, wq_ref, gq_ref, k1_ref, k2_ref, xn_ref, idx_ref, gate_ref, xn_scr):
    f32, bf16 = jnp.float32, jnp.bfloat16

    @pl.when(pl.program_id(1) == 0)
    def _():
        x = x_ref[...]
        y = x * lax.rsqrt(jnp.mean(x * x, -1, keepdims=True) + EPS) * g_ref[...]
        xn_ref[...] = y
        xn_scr[...] = y.astype(bf16)

    q = jnp.dot(xn_scr[...], wq_ref[...], preferred_element_type=f32)
    qn = (q * lax.rsqrt(jnp.mean(q * q, -1, keepdims=True) + EPS) * gq_ref[...]).astype(bf16)
    half = PEER_QDIM // 2
    nt = (((1,), (1,)), ((), ()))
    s1 = lax.dot_general(k1_ref[...], qn[:, :half], nt, preferred_element_type=f32)
    s2 = lax.dot_general(k2_ref[...], qn[:, half:], nt, preferred_element_type=f32)
    for c in range(s1.shape[1] // PEER_LANE_TILE):
        cs = slice(c * PEER_LANE_TILE, (c + 1) * PEER_LANE_TILE)
        v1, i1 = _topk_rows(s1[:, cs], PEER_TOPK)
        v2, i2 = _topk_rows(s2[:, cs], PEER_TOPK)
        cand = jnp.concatenate([v1[a:a + 1, :] + v2 for a in range(PEER_TOPK)], 0)
        vals, ci = _topk_rows(cand, PEER_TOPK)
        a = jnp.floor(ci * (1.0 / PEER_TOPK))
        b = ci - a * PEER_TOPK
        e = _select_rows(i1, a) * float(PEER_KEYS) + _select_rows(i2, b)
        ex = jnp.exp(vals - vals[0:1, :])
        idx_ref[:, cs] = e.astype(jnp.int32)
        gate_ref[:, cs] = ex / jnp.sum(ex, axis=0, keepdims=True)


def _peer_route(x, g, w_q, g_q, sub_keys):
    t, d = x.shape
    tb = PEER_ROUTE_TOKENS
    assert t % tb == 0
    bf16 = jnp.bfloat16
    half = PEER_QDIM // 2
    return pl.pallas_call(
        _peer_route_kernel,
        out_shape=(jax.ShapeDtypeStruct((t, d), jnp.float32),
                   jax.ShapeDtypeStruct((PEER_PAIRS, t), jnp.int32),
                   jax.ShapeDtypeStruct((PEER_PAIRS, t), jnp.float32)),
        grid=(t // tb, PEER_HEADS),
        in_specs=[pl.BlockSpec((tb, d), lambda i, h: (i, 0)),
                  pl.BlockSpec((1, d), lambda i, h: (0, 0)),
                  pl.BlockSpec((d, PEER_QDIM), lambda i, h: (0, h)),
                  pl.BlockSpec((1, PEER_QDIM), lambda i, h: (0, 0)),
                  pl.BlockSpec((PEER_KEYS, half), lambda i, h: (0, 0)),
                  pl.BlockSpec((PEER_KEYS, half), lambda i, h: (0, 0))],
        out_specs=(pl.BlockSpec((tb, d), lambda i, h: (i, 0)),
                   pl.BlockSpec((PEER_TOPK, tb), lambda i, h: (h, i)),
                   pl.BlockSpec((PEER_TOPK, tb), lambda i, h: (h, i))),
        scratch_shapes=[pltpu.VMEM((tb, d), bf16)],
        compiler_params=pltpu.CompilerParams(dimension_semantics=("arbitrary", "arbitrary")),
        name="peer_route",
    )(x, g.reshape(1, d), w_q.astype(bf16), g_q.reshape(1, PEER_QDIM),
      sub_keys[0].astype(bf16), sub_keys[1].astype(bf16))


def _peer_expert_kernel(idx_ref, idx_next_ref, idx_next2_ref, x_ref, gate_ref, res_ref, diag_ref, fold_ref,
                        spread_ref, uv_hbm, out_ref, buf0, buf1, buf2, sems):
    f32, bf16 = jnp.float32, jnp.bfloat16
    tb = PEER_GATHER_TOKENS
    sub = PEER_ROW_TILE[0]
    rows = tb * PEER_PAIRS
    i = pl.program_id(0)
    n = pl.num_programs(0)
    bufs = (buf0, buf1, buf2)

    def start_gather(ids_ref, j):
        for t in range(tb):
            for k in range(PEER_PAIRS):
                pltpu.make_async_copy(uv_hbm.at[ids_ref[t, k]], bufs[j].at[t * PEER_PAIRS + k], sems.at[j]).start(
                    priority=k % PEER_DMA_QUEUES)

    def wait_gather(j):
        pltpu.make_async_copy(uv_hbm.at[pl.ds(0, rows)], bufs[j], sems.at[j]).wait()

    def mix(cur):
        nt = (((1,), (1,)), ((), ()))
        diag = diag_ref[...]
        flat = (PEER_PAIRS * sub, PEER_ROW_TILE[1])

        def slab_rows(t, first):
            return cur[t * PEER_PAIRS:(t + 1) * PEER_PAIRS, first:first + sub, :].reshape(flat).astype(bf16)

        e = jnp.concatenate([lax.dot_general(x_ref[t].astype(bf16), slab_rows(t, 0), nt,
                                             preferred_element_type=f32) * diag for t in range(tb)], 0)
        e_hi = e.astype(bf16)
        e_lo = (e - e_hi.astype(f32)).astype(bf16)
        h = jnp.dot(jnp.concatenate([e_hi, e_lo], 0), fold_ref[...], preferred_element_type=f32)
        h = h[:tb * sub] + h[tb * sub:]
        h = jnp.sum(h.reshape(tb, sub, PEER_PAIRS), axis=1)
        act = 0.5 * h * (1.0 + lax.erf(h * math.sqrt(0.5)))
        w = (gate_ref[...] * act).astype(bf16)
        wrep = jnp.broadcast_to(w[:, None, :], (tb, sub, PEER_PAIRS)).reshape(tb * sub, PEER_PAIRS)
        wfull = jnp.dot(wrep, spread_ref[...], preferred_element_type=f32)
        for t in range(tb):
            wm = (wfull[t * sub:(t + 1) * sub] * diag).astype(bf16)
            out_ref[t] = res_ref[t] + jnp.dot(wm, slab_rows(t, sub), preferred_element_type=f32)

    @pl.when(i == 0)
    def _():
        start_gather(idx_ref, 0)
        start_gather(idx_next_ref, 1)

    for r in range(3):
        @pl.when(i % 3 == r)
        def _(r=r):
            wait_gather(r)
            start_gather(idx_next2_ref, (r + 2) % 3)
            mix(bufs[r])

            @pl.when(i == n - 1)
            def _():
                wait_gather((r + 1) % 3)
                wait_gather((r + 2) % 3)


def _peer_experts(xn, res, idx, gates, u_tab, v_tab):
    t, d = xn.shape
    tb = PEER_GATHER_TOKENS
    sub, lanes = PEER_ROW_TILE
    assert t % tb == 0 and d == sub * lanes
    nblk = t // tb
    rows = tb * PEER_PAIRS
    n_exp = u_tab.shape[0]
    uv = jnp.concatenate([u_tab.reshape(n_exp, sub, lanes), v_tab.reshape(n_exp, sub, lanes)], axis=1)
    lane = jnp.arange(d)
    diag = (lane[None, :] % sub == jnp.arange(sub)[:, None]).astype(jnp.float32)
    fold = (lane[:, None] // sub == jnp.arange(PEER_PAIRS)[None, :]).astype(jnp.bfloat16)
    tile_spec = pl.BlockSpec((tb, sub, lanes), lambda i: (i, 0, 0))
    full_spec = lambda a: pl.BlockSpec(a.shape, lambda i: (0, 0))
    ids_spec = lambda ahead: pl.BlockSpec((tb, PEER_PAIRS), lambda i: (jnp.minimum(i + ahead, nblk - 1), 0),
                                          memory_space=pltpu.SMEM)
    slab_buf = pltpu.VMEM((rows, 2 * sub, lanes), jnp.float32)
    out = pl.pallas_call(
        _peer_expert_kernel,
        out_shape=jax.ShapeDtypeStruct((t, sub, lanes), jnp.float32),
        grid=(nblk,),
        in_specs=[ids_spec(0), ids_spec(1), ids_spec(2),
                  tile_spec,
                  pl.BlockSpec((tb, PEER_PAIRS), lambda i: (i, 0)),
                  tile_spec,
                  full_spec(diag), full_spec(fold), full_spec(fold.T),
                  pl.BlockSpec(memory_space=pl.ANY)],
        out_specs=tile_spec,
        scratch_shapes=[slab_buf, slab_buf, slab_buf, pltpu.SemaphoreType.DMA((3,))],
        compiler_params=pltpu.CompilerParams(dimension_semantics=("arbitrary",), vmem_limit_bytes=VMEM_LIMIT),
        name="peer_experts",
    )(idx, idx, idx, xn.reshape(t, sub, lanes), gates, res.reshape(t, sub, lanes), diag, fold, fold.T, uv)
    return out.reshape(t, d)


def peer(h, g, w_q, g_q, sub_keys, u_tab, v_tab):
    xn, idx_t, gates_t = _peer_route(h, g, w_q, g_q, sub_keys)
    return _peer_experts(xn, h, idx_t.T, gates_t.T, u_tab, v_tab)


def kernel(x_prompt, x_sample, p_prompt, p_sample, state_ssm, state_conv, cache_ckv, cache_krope, page_table,
           ln_mix, ln_ffn, ln_ple,
           m_w_in, m_conv_w, m_conv_b, m_dt_bias, m_a_log, m_d, m_g_norm, m_w_out,
           kv_g_in, kv_w_dkv, kv_g_ckv, kv_w_kr, kv_g_kr, kv_w_uk, kv_g_kn, kv_w_uv,
           q_w_dq, q_g_cq, q_w_uq, q_g_qn, q_g_qr, a_w_o,
           peer_w_q, peer_g_q, peer_sub_keys, peer_u, peer_v,
           ple_w_proj, ple_w_gate):
    bf16 = jnp.bfloat16
    bp, lp = x_prompt.shape[:2]
    bs, ls = x_sample.shape[:2]
    n_p, n_s = bp * lp, bs * ls
    pos_p = jnp.arange(lp)
    pos_s = page_table.shape[1] * PAGE_SIZE + jnp.arange(ls)
    h = jnp.concatenate([x_prompt.reshape(n_p, D_MODEL), x_sample.reshape(n_s, D_MODEL)], 0)
    p_all = jnp.concatenate([p_prompt.reshape(DEPTH, n_p, PLE_DIM), p_sample.reshape(DEPTH, n_s, PLE_DIM)], 1)

    def split(a, tail):
        return a[:n_p].reshape((bp, lp) + tail), a[n_p:].reshape((bs, ls) + tail)

    ssm_p, conv_p, ssm_s, conv_s = [], [], [], []
    for i in range(DEPTH):
        if i < N_A:
            w_in = m_w_in[i].astype(bf16)
            z, xbc, dt = _dense(h, [w_in[:, :D_INNER], w_in[:, D_INNER:D_INNER + CONV_DIM],
                                    w_in[:, D_INNER + CONV_DIM:]], gain=ln_mix[i])
            mw = (m_conv_w[i], m_conv_b[i], m_dt_bias[i], m_a_log[i], m_d[i], m_g_norm[i])
            (z_p, z_s), (xbc_p, xbc_s), (dt_p, dt_s) = (split(z, (D_INNER,)), split(xbc, (CONV_DIM,)),
                                                        split(dt, (SSM_HEADS,)))
            conv0 = jnp.zeros((bp, CONV_W - 1, CONV_DIM), h.dtype)
            ssm0 = jnp.zeros((bp, SSM_HEADS, SSM_HEADDIM, SSM_STATE), h.dtype)
            y_p, c_p, s_p = mamba_sequence(z_p, xbc_p, dt_p, conv0, ssm0, *mw)
            y_s, c_s, s_s = mamba_sequence(z_s, xbc_s, dt_s, state_conv[i], state_ssm[i], *mw)
            conv_p.append(c_p)
            ssm_p.append(s_p)
            conv_s.append(c_s)
            ssm_s.append(s_s)
            y = jnp.concatenate([y_p.reshape(n_p, D_INNER), y_s.reshape(n_s, D_INNER)], 0)
            (h,) = _dense(y, [m_w_out[i].astype(bf16)], residual=h)
        else:
            if i == N_A:
                ckv_raw, kr_raw = _dense(h, [kv_w_dkv.astype(bf16), kv_w_kr.astype(bf16)], gain=kv_g_in)
                ckv = rms_norm(ckv_raw, kv_g_ckv)
                ckv_p, ckv_s = split(ckv, (KV_LORA,))
                kr_p, kr_s = split(rms_norm(kr_raw, kv_g_kr), (ROPE_DIM,))
                kr_p, kr_s = rope(kr_p, pos_p), rope(kr_s, pos_s)
                k_raw, v_p = _dense(ckv[:n_p], [kv_w_uk.astype(bf16), kv_w_uv.astype(bf16)])
                kn_p = rms_norm(k_raw.reshape(bp, lp, MLA_HEADS, NOPE_DIM), kv_g_kn)
                v_p = v_p.reshape(bp, lp, MLA_HEADS, V_DIM)
            j = i - N_A
            (cq_raw,) = _dense(h, [q_w_dq[j].astype(bf16)], gain=ln_mix[i])
            (q,) = _dense(cq_raw, [q_w_uq[j].astype(bf16)], gain=q_g_cq[j])
            q = q.reshape(n_p + n_s, MLA_HEADS, NOPE_DIM + ROPE_DIM)
            qn_p, qn_s = split(rms_norm(q[..., :NOPE_DIM], q_g_qn[j]), (MLA_HEADS, NOPE_DIM))
            qr_p, qr_s = split(rms_norm(q[..., NOPE_DIM:], q_g_qr[j]), (MLA_HEADS, ROPE_DIM))
            qr_p, qr_s = rope(qr_p, pos_p), rope(qr_s, pos_s)
            o_p = mla_attend_prompt(qn_p, qr_p, kn_p, kr_p, v_p)
            o_s = mla_attend_sample(qn_s, qr_s, ckv_s, kr_s, cache_ckv, cache_krope, page_table,
                                    kv_w_uk, kv_g_kn, kv_w_uv)
            o = jnp.concatenate([o_p.reshape(n_p, MLA_HEADS * V_DIM), o_s.reshape(n_s, MLA_HEADS * V_DIM)], 0)
            (h,) = _dense(o, [a_w_o[j].astype(bf16)], residual=h)
        h = peer(h, ln_ffn[i], peer_w_q[i], peer_g_q[i], peer_sub_keys[i], peer_u[i], peer_v[i])
        h = _ple(h, p_all[i], ln_ple[i], ple_w_proj[i], ple_w_gate[i])
    hp, hs = split(h, (D_MODEL,))
    return (hp, hs, jnp.stack(ssm_p), jnp.stack(conv_p), ckv_p, kr_p,
            jnp.stack(ssm_s), jnp.stack(conv_s), ckv_s, kr_s)
```

```python
import functools
import math

import jax
import jax.numpy as jnp
from jax import lax
from jax.experimental import pallas as pl
from jax.experimental.pallas import tpu as pltpu

D_MODEL = 1024
DEPTH = 2
PAGE_SIZE = 128
N_A = DEPTH // 2
D_INNER = 2 * D_MODEL
SSM_HEADDIM = 64
SSM_HEADS = D_INNER // SSM_HEADDIM
SSM_GROUPS = 4
SSM_STATE = 128
CONV_W = 4
CONV_DIM = D_INNER + 2 * SSM_GROUPS * SSM_STATE
SSD_CHUNK = 128
MLA_HEADS = 16
Q_LORA = 384
KV_LORA = 256
NOPE_DIM = 64
ROPE_DIM = 32
V_DIM = 64
ROPE_THETA = 10000.0
ATTN_SCALE = (NOPE_DIM + ROPE_DIM) ** -0.5
PEER_HEADS = 8
PEER_KEYS = 128
PEER_QDIM = 256
PEER_TOPK = 16
PLE_DIM = 256
EPS = 1e-6

VMEM_LIMIT = 48 * 1024 * 1024


def rms_norm(x, g):
    xf = x.astype(jnp.float32)
    y = xf * lax.rsqrt(jnp.mean(xf * xf, -1, keepdims=True) + EPS)
    return (y * g.astype(jnp.float32)).astype(x.dtype)


def rope(x, pos):
    half = ROPE_DIM // 2
    inv = ROPE_THETA ** (-jnp.arange(half, dtype=jnp.float32) / half)
    ang = pos.astype(jnp.float32)[:, None] * inv
    ang = ang.reshape((1, ang.shape[0]) + (1,) * (x.ndim - 3) + (half,))
    cos, sin = jnp.cos(ang), jnp.sin(ang)
    xf = x.astype(jnp.float32)
    x1, x2 = xf[..., :half], xf[..., half:]
    return jnp.concatenate([x1 * cos - x2 * sin, x1 * sin + x2 * cos], -1).astype(x.dtype)


DENSE_ROWS = 256


def _dense_kernel(*refs, n_out, has_gain, has_residual):
    refs = list(refs)
    x_ref = refs.pop(0)
    g_ref = refs.pop(0) if has_gain else None
    w_refs = [refs.pop(0) for _ in range(n_out)]
    r_ref = refs.pop(0) if has_residual else None
    x = x_ref[...]
    if has_gain:
        x = x * lax.rsqrt(jnp.mean(x * x, -1, keepdims=True) + EPS) * g_ref[...]
    xb = x.astype(jnp.bfloat16)
    for w_ref, o_ref in zip(w_refs, refs):
        y = jnp.dot(xb, w_ref[...], preferred_element_type=jnp.float32)
        o_ref[...] = y + r_ref[...] if has_residual else y


def _dense(x, weights, gain=None, residual=None):
    rows, k = x.shape
    tm = math.gcd(rows, DENSE_ROWS)
    assert residual is None or len(weights) == 1
    row_spec = lambda n: pl.BlockSpec((tm, n), lambda i: (i, 0))
    full_spec = lambda a: pl.BlockSpec(a.shape, lambda i: (0, 0))
    args, specs = [x], [row_spec(k)]
    if gain is not None:
        args.append(gain.reshape(1, k).astype(jnp.float32))
        specs.append(full_spec(args[-1]))
    for w in weights:
        args.append(w)
        specs.append(full_spec(w))
    if residual is not None:
        args.append(residual)
        specs.append(row_spec(residual.shape[1]))
    return pl.pallas_call(
        functools.partial(_dense_kernel, n_out=len(weights), has_gain=gain is not None,
                          has_residual=residual is not None),
        out_shape=tuple(jax.ShapeDtypeStruct((rows, w.shape[1]), jnp.float32) for w in weights),
        grid=(rows // tm,),
        in_specs=specs,
        out_specs=tuple(row_spec(w.shape[1]) for w in weights),
        compiler_params=pltpu.CompilerParams(dimension_semantics=("arbitrary",), vmem_limit_bytes=VMEM_LIMIT),
        name="dense",
    )(*args)


def _ple_kernel(h_ref, p_ref, g_ref, wp_ref, wg_ref, o_ref):
    f32, bf16 = jnp.float32, jnp.bfloat16
    h = h_ref[...]
    hn = h * lax.rsqrt(jnp.mean(h * h, -1, keepdims=True) + EPS) * g_ref[...]
    gate = jax.nn.sigmoid(jnp.dot(hn.astype(bf16), wg_ref[...], preferred_element_type=f32))
    proj = jnp.dot(p_ref[...].astype(bf16), wp_ref[...], preferred_element_type=f32)
    o_ref[...] = h + proj * gate


def _ple(h, p, g, w_proj, w_gate):
    rows, d = h.shape
    tm = math.gcd(rows, DENSE_ROWS)
    row_spec = lambda n: pl.BlockSpec((tm, n), lambda i: (i, 0))
    full_spec = lambda a: pl.BlockSpec(a.shape, lambda i: (0, 0))
    g2 = g.reshape(1, d).astype(jnp.float32)
    wp, wg = w_proj.astype(jnp.bfloat16), w_gate.astype(jnp.bfloat16)
    return pl.pallas_call(
        _ple_kernel,
        out_shape=jax.ShapeDtypeStruct((rows, d), jnp.float32),
        grid=(rows // tm,),
        in_specs=[row_spec(d), row_spec(p.shape[1]), full_spec(g2), full_spec(wp), full_spec(wg)],
        out_specs=row_spec(d),
        compiler_params=pltpu.CompilerParams(dimension_semantics=("arbitrary",), vmem_limit_bytes=VMEM_LIMIT),
        name="ple",
    )(h, p, g2, wp, wg)


SSD_TAIL = 8


def _ssd_kernel(z_ref, xbc_ref, dt_ref, prev_ref, h0_ref, cw_ref, cb_ref, dtb_ref, a_ref, dskip_ref, gn_ref,
                expand_ref, expand_t_ref, y_ref, hout_ref, ext_scr, h_scr, *, n_valid):
    f32, bf16 = jnp.float32, jnp.bfloat16
    hi = lax.Precision.HIGHEST
    q = z_ref.shape[0]
    c = pl.program_id(1)
    heads_per_group = SSM_HEADS // SSM_GROUPS
    gw = heads_per_group * SSM_HEADDIM
    nt = (((1,), (1,)), ((), ()))
    tn = (((0,), (0,)), ((), ()))

    @pl.when(c == 0)
    def _():
        ext_scr[0:SSD_TAIL, :] = prev_ref[...]
        h_scr[...] = h0_ref[...]

    ext_scr[SSD_TAIL:SSD_TAIL + q, :] = xbc_ref[...]
    conv = cb_ref[...]
    for w in range(CONV_W):
        conv = conv + ext_scr[pl.ds(SSD_TAIL - (CONV_W - 1) + w, q), :] * cw_ref[w:w + 1, :]
    tail = ext_scr[q:q + SSD_TAIL, :]
    ext_scr[0:SSD_TAIL, :] = tail
    act = conv * jax.nn.sigmoid(conv)
    xs = act[:, :D_INNER]
    bmat = act[:, D_INNER:D_INNER + SSM_GROUPS * SSM_STATE].astype(bf16)
    cmat = act[:, D_INNER + SSM_GROUPS * SSM_STATE:].astype(bf16)

    row = lax.broadcasted_iota(jnp.int32, (q, q), 0)
    col = lax.broadcasted_iota(jnp.int32, (q, q), 1)
    dt = jax.nn.softplus(dt_ref[...] + dtb_ref[...])
    if n_valid < q:
        dt = jnp.where(lax.broadcasted_iota(jnp.int32, dt.shape, 0) < n_valid, dt, 0.0)
    a = dt * a_ref[...]
    a_cum = jnp.dot((col <= row).astype(f32), a, precision=hi, preferred_element_type=f32)
    a_cum_t = jnp.dot(a.T, (row <= col).astype(f32), precision=hi, preferred_element_type=f32)
    a_last = a_cum[q - 1:q, :]
    expand = expand_ref[...]
    widen = lambda v: jnp.dot(v, expand, precision=hi, preferred_element_type=f32)
    xd = xs * widen(dt)
    xd_b = xd.astype(bf16)
    xdd_b = (xd * widen(jnp.exp(a_last - a_cum))).astype(bf16)
    grow = widen(jnp.exp(a_cum))
    carry = jnp.dot(expand_t_ref[...], jnp.exp(a_cum_t[:, q - 1:q]), precision=hi,
                    preferred_element_type=f32)

    y_parts = []
    for g in range(SSM_GROUPS):
        bg = bmat[:, g * SSM_STATE:(g + 1) * SSM_STATE]
        cg = cmat[:, g * SSM_STATE:(g + 1) * SSM_STATE]
        cbg = lax.dot_general(cg, bg, nt, preferred_element_type=f32)
        h_in = h_scr[g * gw:(g + 1) * gw, :]
        y_off = lax.dot_general(cg, h_in.astype(bf16), nt, preferred_element_type=f32)
        y_g = y_off * grow[:, g * gw:(g + 1) * gw]
        diag_parts = []
        for r in range(heads_per_group):
            hd = g * heads_per_group + r
            seg = a_cum[:, hd:hd + 1] - a_cum_t[hd:hd + 1, :]
            lmat = jnp.exp(jnp.where(col <= row, seg, -jnp.inf))
            lanes = slice(hd * SSM_HEADDIM, (hd + 1) * SSM_HEADDIM)
            diag_parts.append(jnp.dot((cbg * lmat).astype(bf16), xd_b[:, lanes], preferred_element_type=f32))
        y_parts.append(y_g + jnp.concatenate(diag_parts, 1))
        st = lax.dot_general(xdd_b[:, g * gw:(g + 1) * gw], bg, tn, preferred_element_type=f32)
        h_scr[g * gw:(g + 1) * gw, :] = h_in * carry[g * gw:(g + 1) * gw, :] + st
    y = jnp.concatenate(y_parts, 1) + dskip_ref[...] * xs
    zz = z_ref[...]
    y = y * (zz * jax.nn.sigmoid(zz))
    gsz = D_INNER // SSM_GROUPS
    outs = []
    for g in range(SSM_GROUPS):
        seg = y[:, g * gsz:(g + 1) * gsz]
        outs.append(seg * lax.rsqrt(jnp.mean(seg * seg, -1, keepdims=True) + EPS))
    y_ref[...] = jnp.concatenate(outs, 1) * gn_ref[...]

    @pl.when(c == pl.num_programs(1) - 1)
    def _():
        hout_ref[...] = h_scr[...]


def _ssd(z, xbc, dt, conv_prev, h0, conv_w, conv_b, dt_bias, a_log, d_skip, g_norm, chunk, n_valid):
    b, l, _ = z.shape
    f32 = jnp.float32
    assert l % chunk == 0
    prev = jnp.pad(conv_prev.astype(f32), ((0, 0), (SSD_TAIL - (CONV_W - 1), 0), (0, 0)))
    rows = SSM_HEADS * SSM_HEADDIM
    expand = jnp.repeat(jnp.eye(SSM_HEADS, dtype=f32), SSM_HEADDIM, axis=1)
    vec = lambda v, n: v.astype(f32).reshape(1, n)
    seq_spec = lambda n: pl.BlockSpec((None, chunk, n), lambda bi, ci: (bi, ci, 0))
    bat_spec = lambda r, n: pl.BlockSpec((None, r, n), lambda bi, ci: (bi, 0, 0))
    full_spec = lambda a: pl.BlockSpec(a.shape, lambda bi, ci: (0, 0))
    params = [conv_w.astype(f32), vec(conv_b, CONV_DIM), vec(dt_bias, SSM_HEADS), vec(-jnp.exp(a_log.astype(f32)), SSM_HEADS),
              vec(jnp.repeat(d_skip, SSM_HEADDIM), D_INNER), vec(g_norm, D_INNER), expand, expand.T]
    y, h_new = pl.pallas_call(
        functools.partial(_ssd_kernel, n_valid=n_valid),
        out_shape=(jax.ShapeDtypeStruct((b, l, D_INNER), f32), jax.ShapeDtypeStruct((b, rows, SSM_STATE), f32)),
        grid=(b, l // chunk),
        in_specs=[seq_spec(D_INNER), seq_spec(CONV_DIM), seq_spec(SSM_HEADS), bat_spec(SSD_TAIL, CONV_DIM),
                  bat_spec(rows, SSM_STATE)] + [full_spec(p) for p in params],
        out_specs=(seq_spec(D_INNER), bat_spec(rows, SSM_STATE)),
        scratch_shapes=[pltpu.VMEM((chunk + SSD_TAIL, CONV_DIM), f32), pltpu.VMEM((rows, SSM_STATE), f32)],
        compiler_params=pltpu.CompilerParams(dimension_semantics=("arbitrary", "arbitrary"),
                                             vmem_limit_bytes=VMEM_LIMIT),
        name="ssd",
    )(z, xbc, dt, prev, h0.astype(f32).reshape(b, rows, SSM_STATE), *params)
    return y, h_new.reshape(b, SSM_HEADS, SSM_HEADDIM, SSM_STATE)


SSD_PAD_ROWS = 16


def mamba_sequence(z, xbc, dt, conv_prev, h0, conv_w, conv_b, dt_bias, a_log, d_skip, g_norm):
    b, l, _ = z.shape
    conv_new = jnp.concatenate([conv_prev.astype(xbc.dtype), xbc], 1)[:, -(CONV_W - 1):]
    mw = (conv_w, conv_b, dt_bias, a_log, d_skip, g_norm)
    if l % SSD_CHUNK == 0:
        y, h_new = _ssd(z, xbc, dt, conv_prev, h0, *mw, chunk=SSD_CHUNK, n_valid=SSD_CHUNK)
    else:
        assert l <= SSD_PAD_ROWS
        pad = lambda a: jnp.pad(a, ((0, 0), (0, SSD_PAD_ROWS - l), (0, 0)))
        y, h_new = _ssd(pad(z), pad(xbc), pad(dt), conv_prev, h0, *mw, chunk=SSD_PAD_ROWS, n_valid=l)
        y = y[:, :l]
    return y, conv_new, h_new.astype(z.dtype)


FLASH_TILE = 1024
FLASH_LANES = 128


def _flash_kernel(q_ref, k_ref, vt_ref, o_ref, m_scr, l_scr, acc_scr):
    f32 = jnp.float32
    qi, ki = pl.program_id(2), pl.program_id(3)

    @pl.when(ki == 0)
    def _():
        m_scr[...] = jnp.full(m_scr.shape, -jnp.inf, f32)
        l_scr[...] = jnp.zeros(l_scr.shape, f32)
        acc_scr[...] = jnp.zeros(acc_scr.shape, f32)

    def update(on_diagonal):
        s = lax.dot_general(k_ref[...], q_ref[...], (((1,), (1,)), ((), ())), preferred_element_type=f32)
        s = s * ATTN_SCALE
        if on_diagonal:
            key = lax.broadcasted_iota(jnp.int32, s.shape, 0)
            query = lax.broadcasted_iota(jnp.int32, s.shape, 1)
            s = jnp.where(key <= query, s, -jnp.inf)
        m_prev = m_scr[...]
        m_new = jnp.maximum(m_prev, jnp.max(s, axis=0, keepdims=True))
        alpha = jnp.exp(m_prev - m_new)
        p = jnp.exp(s - m_new)
        l_scr[...] = alpha * l_scr[...] + jnp.sum(p, axis=0, keepdims=True)
        acc_scr[...] = alpha * acc_scr[...] + jnp.dot(vt_ref[...], p.astype(vt_ref.dtype),
                                                      preferred_element_type=f32)
        m_scr[...] = m_new

    @pl.when(ki < qi)
    def _():
        update(False)

    @pl.when(ki == qi)
    def _():
        update(True)
        o_ref[...] = acc_scr[...] / l_scr[...]


def _flash_attention(q, k, vt):
    b, h, l, _ = q.shape
    t = min(FLASH_TILE, l)
    assert l % t == 0
    nblk = l // t
    return pl.pallas_call(
        _flash_kernel,
        out_shape=jax.ShapeDtypeStruct((b, h, V_DIM, l), jnp.float32),
        grid=(b, h, nblk, nblk),
        in_specs=[pl.BlockSpec((None, None, t, FLASH_LANES), lambda bi, hi, qi, ki: (bi, hi, qi, 0)),
                  pl.BlockSpec((None, None, t, FLASH_LANES),
                               lambda bi, hi, qi, ki: (bi, hi, jnp.minimum(ki, qi), 0)),
                  pl.BlockSpec((None, None, V_DIM, t), lambda bi, hi, qi, ki: (bi, hi, 0, jnp.minimum(ki, qi)))],
        out_specs=pl.BlockSpec((None, None, V_DIM, t), lambda bi, hi, qi, ki: (bi, hi, 0, qi)),
        scratch_shapes=[pltpu.VMEM((1, t), jnp.float32), pltpu.VMEM((1, t), jnp.float32),
                        pltpu.VMEM((V_DIM, t), jnp.float32)],
        compiler_params=pltpu.CompilerParams(
            dimension_semantics=("arbitrary", "arbitrary", "arbitrary", "arbitrary"),
            vmem_limit_bytes=VMEM_LIMIT),
        name="prompt_attention",
    )(q, k, vt)


def mla_attend_prompt(qn, qr, kn, kr, v):
    b, l = qn.shape[:2]
    bf16 = jnp.bfloat16
    pad = jnp.zeros((b, l, MLA_HEADS, FLASH_LANES - NOPE_DIM - ROPE_DIM), bf16)
    q_cat = jnp.concatenate([qn.astype(bf16), qr.astype(bf16), pad], -1).transpose(0, 2, 1, 3)
    kr_h = jnp.broadcast_to(kr.astype(bf16)[:, :, None, :], (b, l, MLA_HEADS, ROPE_DIM))
    k_cat = jnp.concatenate([kn.astype(bf16), kr_h, pad], -1).transpose(0, 2, 1, 3)
    o = _flash_attention(q_cat, k_cat, v.astype(bf16).transpose(0, 2, 3, 1))
    return o.transpose(0, 3, 1, 2).reshape(b, l, MLA_HEADS * V_DIM)


DECODE_PAGES = 8
DECODE_NEW_ROWS = 8


def _decode_kernel(pt_ref, *refs):
    f32, bf16 = jnp.float32, jnp.bfloat16
    npg = DECODE_PAGES
    c_refs, kr_refs = refs[:npg], refs[npg:2 * npg]
    (qn_ref, qr_ref, cnew_ref, krnew_ref, wuk_ref, ind_ref, o_ref, m_scr, l_scr, acc_scr) = refs[2 * npg:]
    j = pl.program_id(1)
    nt = (((1,), (1,)), ((), ()))
    rows = qn_ref.shape[0]

    @pl.when(j == 0)
    def _():
        m_scr[...] = jnp.full(m_scr.shape, -jnp.inf, f32)
        l_scr[...] = jnp.zeros(l_scr.shape, f32)
        acc_scr[...] = jnp.zeros(acc_scr.shape, f32)

    def scores(c, kr):
        cb = c.astype(bf16)
        k = jnp.dot(cb, wuk_ref[...], preferred_element_type=f32)
        ms = lax.dot_general(ind_ref[...], (k * k).astype(bf16), nt, preferred_element_type=f32)
        r = lax.rsqrt(ms[:rows] * (1.0 / NOPE_DIM) + EPS)
        s = r * lax.dot_general(qn_ref[...], k.astype(bf16), nt, preferred_element_type=f32)
        s = s + lax.dot_general(qr_ref[...], kr.astype(bf16), nt, preferred_element_type=f32)
        return s * ATTN_SCALE, cb

    def update(s, cb):
        m_prev = m_scr[...]
        m_new = jnp.maximum(m_prev, jnp.max(s, axis=-1, keepdims=True))
        alpha = jnp.exp(m_prev - m_new)
        p = jnp.exp(s - m_new)
        l_scr[...] = alpha * l_scr[...] + jnp.sum(p, axis=-1, keepdims=True)
        acc_scr[...] = alpha * acc_scr[...] + jnp.dot(p.astype(bf16), cb, preferred_element_type=f32)
        m_scr[...] = m_new

    c_all = jnp.concatenate([r[...] for r in c_refs], 0)
    kr_all = jnp.concatenate([r[...] for r in kr_refs], 0)
    update(*scores(c_all, kr_all))

    @pl.when(j == pl.num_programs(1) - 1)
    def _():
        s, cb = scores(cnew_ref[...], krnew_ref[...])
        key = lax.broadcasted_iota(jnp.int32, s.shape, 1)
        query = lax.broadcasted_iota(jnp.int32, s.shape, 0) // MLA_HEADS
        update(jnp.where(key <= query, s, -jnp.inf), cb)
        o_ref[...] = acc_scr[...] / l_scr[...]


def mla_attend_sample(qn, qr, ckv_new, kr_new, cache_ckv, cache_krope, page_table, w_uk, g_kn, w_uv):
    b, lq = qn.shape[:2]
    f32, bf16 = jnp.float32, jnp.bfloat16
    n_pages = page_table.shape[1]
    npg = DECODE_PAGES
    assert n_pages % npg == 0 and lq <= DECODE_NEW_ROWS
    rows = lq * MLA_HEADS
    hd = MLA_HEADS * NOPE_DIM
    eye = jnp.eye(MLA_HEADS, dtype=f32)
    qg = qn.astype(f32) * g_kn.astype(f32)
    qn_bd = (qg[:, :, :, None, :] * eye[None, None, :, :, None]).reshape(b, rows, hd).astype(bf16)
    qr_rows = qr.reshape(b, rows, ROPE_DIM).astype(bf16)
    ind = jnp.tile(jnp.repeat(eye, NOPE_DIM, axis=1), (lq, 1))
    ind = jnp.pad(ind, ((0, -rows % 128), (0, 0))).astype(bf16)
    pad_new = ((0, 0), (0, DECODE_NEW_ROWS - lq), (0, 0))
    cnew = jnp.pad(ckv_new.astype(f32), pad_new)
    krnew = jnp.pad(kr_new.astype(f32), pad_new)

    def page_spec(width, i):
        return pl.BlockSpec((None, PAGE_SIZE, width), lambda bi, j, pt: (pt[bi, j * npg + i], 0, 0))

    def batch_spec(shape):
        return pl.BlockSpec((None,) + shape, lambda bi, j, pt: (bi, 0, 0))

    grid_spec = pltpu.PrefetchScalarGridSpec(
        num_scalar_prefetch=1,
        grid=(b, n_pages // npg),
        in_specs=([page_spec(KV_LORA, i) for i in range(npg)] + [page_spec(ROPE_DIM, i) for i in range(npg)] +
                  [batch_spec((rows, hd)), batch_spec((rows, ROPE_DIM)),
                   batch_spec((DECODE_NEW_ROWS, KV_LORA)), batch_spec((DECODE_NEW_ROWS, ROPE_DIM)),
                   pl.BlockSpec((KV_LORA, hd), lambda bi, j, pt: (0, 0)),
                   pl.BlockSpec(ind.shape, lambda bi, j, pt: (0, 0))]),
        out_specs=batch_spec((rows, KV_LORA)),
        scratch_shapes=[pltpu.VMEM((rows, 1), f32), pltpu.VMEM((rows, 1), f32), pltpu.VMEM((rows, KV_LORA), f32)],
    )
    o_lat = pl.pallas_call(
        _decode_kernel,
        grid_spec=grid_spec,
        out_shape=jax.ShapeDtypeStruct((b, rows, KV_LORA), f32),
        compiler_params=pltpu.CompilerParams(dimension_semantics=("arbitrary", "arbitrary"),
                                             vmem_limit_bytes=VMEM_LIMIT),
        name="sample_attention",
    )(page_table, *([cache_ckv] * npg), *([cache_krope] * npg), qn_bd, qr_rows, cnew, krnew,
      w_uk.astype(bf16), ind)
    o_lat = o_lat.reshape(b, lq, MLA_HEADS, KV_LORA)
    o = jnp.einsum('bqhc,chd->bqhd', o_lat, w_uv.astype(f32).reshape(KV_LORA, MLA_HEADS, V_DIM))
    return o.reshape(b, lq, MLA_HEADS * V_DIM).astype(qn.dtype)


PEER_ROUTE_TOKENS = 256
PEER_LANE_TILE = 128
PEER_GATHER_TOKENS = 8
PEER_ROW_TILE = (8, 128)
PEER_PAIRS = PEER_HEADS * PEER_TOPK
PEER_DMA_QUEUES = 2


def _topk_rows(s, k, ids=None):
    if ids is None:
        ids = lax.broadcasted_iota(jnp.int32, s.shape, 0).astype(jnp.float32)
    vals, idxs = [], []
    for _ in range(k):
        m = jnp.max(s, axis=0, keepdims=True)
        ix = jnp.min(jnp.where(s == m, ids, jnp.inf), axis=0, keepdims=True)
        vals.append(m)
        idxs.append(ix)
        s = jnp.where(ids == ix, -jnp.inf, s)
    return jnp.concatenate(vals, 0), jnp.concatenate(idxs, 0)


def _pair_candidates(v1, v2):
    sub = 8
    assert PEER_TOPK == 2 * sub
    row = lax.broadcasted_iota(jnp.int32, (sub, v1.shape[1]), 0).astype(jnp.float32)
    lo, hi = v2[:sub], v2[sub:]
    vals = [v1[0:1] + lo, v1[0:1] + hi] + [v1[a:a + 1] + lo for a in range(1, sub)] + [v1[sub:] + v2[0:1]]
    ids = [row, row + sub] + [row + a * PEER_TOPK for a in range(1, sub)] + [(row + sub) * PEER_TOPK]
    return jnp.concatenate(vals, 0), jnp.concatenate(ids, 0)


def _select_rows(table, sel):
    out = jnp.zeros(sel.shape, table.dtype)
    for j in range(table.shape[0]):
        out = jnp.where(sel == float(j), table[j:j + 1, :], out)
    return out


def _peer_route_kernel(x_ref, g_ref, wq_ref, gq_ref, k1_ref, k2_ref, xn_ref, idx_ref, gate_ref, xn_scr):
    f32, bf16 = jnp.float32, jnp.bfloat16

    @pl.when(pl.program_id(1) == 0)
    def _():
        x = x_ref[...]
        y = x * lax.rsqrt(jnp.mean(x * x, -1, keepdims=True) + EPS) * g_ref[...]
        xn_ref[...] = y
        xn_scr[...] = y.astype(bf16)

    q = jnp.dot(xn_scr[...], wq_ref[...], preferred_element_type=f32)
    qn = (q * lax.rsqrt(jnp.mean(q * q, -1, keepdims=True) + EPS) * gq_ref[...]).astype(bf16)
    half = PEER_QDIM // 2
    nt = (((1,), (1,)), ((), ()))
    s1 = lax.dot_general(k1_ref[...], qn[:, :half], nt, preferred_element_type=f32)
    s2 = lax.dot_general(k2_ref[...], qn[:, half:], nt, preferred_element_type=f32)
    for c in range(s1.shape[1] // PEER_LANE_TILE):
        cs = slice(c * PEER_LANE_TILE, (c + 1) * PEER_LANE_TILE)
        v1, i1 = _topk_rows(s1[:, cs], PEER_TOPK)
        v2, i2 = _topk_rows(s2[:, cs], PEER_TOPK)
        cand, cand_ids = _pair_candidates(v1, v2)
        vals, ci = _topk_rows(cand, PEER_TOPK, cand_ids)
        a = jnp.floor(ci * (1.0 / PEER_TOPK))
        b = ci - a * PEER_TOPK
        e = _select_rows(i1, a) * float(PEER_KEYS) + _select_rows(i2, b)
        ex = jnp.exp(vals - vals[0:1, :])
        idx_ref[:, cs] = e.astype(jnp.int32)
        gate_ref[:, cs] = ex / jnp.sum(ex, axis=0, keepdims=True)


def _peer_route(x, g, w_q, g_q, sub_keys):
    t, d = x.shape
    tb = PEER_ROUTE_TOKENS
    assert t % tb == 0
    bf16 = jnp.bfloat16
    half = PEER_QDIM // 2
    return pl.pallas_call(
        _peer_route_kernel,
        out_shape=(jax.ShapeDtypeStruct((t, d), jnp.float32),
                   jax.ShapeDtypeStruct((PEER_PAIRS, t), jnp.int32),
                   jax.ShapeDtypeStruct((PEER_PAIRS, t), jnp.float32)),
        grid=(t // tb, PEER_HEADS),
        in_specs=[pl.BlockSpec((tb, d), lambda i, h: (i, 0)),
                  pl.BlockSpec((1, d), lambda i, h: (0, 0)),
                  pl.BlockSpec((d, PEER_QDIM), lambda i, h: (0, h)),
                  pl.BlockSpec((1, PEER_QDIM), lambda i, h: (0, 0)),
                  pl.BlockSpec((PEER_KEYS, half), lambda i, h: (0, 0)),
                  pl.BlockSpec((PEER_KEYS, half), lambda i, h: (0, 0))],
        out_specs=(pl.BlockSpec((tb, d), lambda i, h: (i, 0)),
                   pl.BlockSpec((PEER_TOPK, tb), lambda i, h: (h, i)),
                   pl.BlockSpec((PEER_TOPK, tb), lambda i, h: (h, i))),
        scratch_shapes=[pltpu.VMEM((tb, d), bf16)],
        compiler_params=pltpu.CompilerParams(dimension_semantics=("arbitrary", "arbitrary")),
        name="peer_route",
    )(x, g.reshape(1, d), w_q.astype(bf16), g_q.reshape(1, PEER_QDIM),
      sub_keys[0].astype(bf16), sub_keys[1].astype(bf16))


def _peer_expert_kernel(idx_ref, idx_next_ref, idx_next2_ref, x_ref, gate_ref, res_ref, diag_ref, fold_ref,
                        spread_ref, uv_hbm, out_ref, buf0, buf1, buf2, sems):
    f32, bf16 = jnp.float32, jnp.bfloat16
    tb = PEER_GATHER_TOKENS
    sub = PEER_ROW_TILE[0]
    rows = tb * PEER_PAIRS
    i = pl.program_id(0)
    n = pl.num_programs(0)
    bufs = (buf0, buf1, buf2)

    def start_gather(ids_ref, j):
        for t in range(tb):
            for k in range(PEER_PAIRS):
                pltpu.make_async_copy(uv_hbm.at[ids_ref[t, k]], bufs[j].at[t * PEER_PAIRS + k], sems.at[j]).start(
                    priority=k % PEER_DMA_QUEUES)

    def wait_gather(j):
        pltpu.make_async_copy(uv_hbm.at[pl.ds(0, rows)], bufs[j], sems.at[j]).wait()

    def mix(cur):
        nt = (((1,), (1,)), ((), ()))
        diag = diag_ref[...]
        flat = (PEER_PAIRS * sub, PEER_ROW_TILE[1])

        def slab_rows(t, first):
            return cur[t * PEER_PAIRS:(t + 1) * PEER_PAIRS, first:first + sub, :].reshape(flat).astype(bf16)

        e = jnp.concatenate([lax.dot_general(x_ref[t].astype(bf16), slab_rows(t, 0), nt,
                                             preferred_element_type=f32) * diag for t in range(tb)], 0)
        e_hi = e.astype(bf16)
        e_lo = (e - e_hi.astype(f32)).astype(bf16)
        h = jnp.dot(jnp.concatenate([e_hi, e_lo], 0), fold_ref[...], preferred_element_type=f32)
        h = h[:tb * sub] + h[tb * sub:]
        h = jnp.sum(h.reshape(tb, sub, PEER_PAIRS), axis=1)
        act = 0.5 * h * (1.0 + lax.erf(h * math.sqrt(0.5)))
        w = (gate_ref[...] * act).astype(bf16)
        wrep = jnp.broadcast_to(w[:, None, :], (tb, sub, PEER_PAIRS)).reshape(tb * sub, PEER_PAIRS)
        wfull = jnp.dot(wrep, spread_ref[...], preferred_element_type=f32)
        for t in range(tb):
            wm = (wfull[t * sub:(t + 1) * sub] * diag).astype(bf16)
            out_ref[t] = res_ref[t] + jnp.dot(wm, slab_rows(t, sub), preferred_element_type=f32)

    @pl.when(i == 0)
    def _():
        start_gather(idx_ref, 0)
        start_gather(idx_next_ref, 1)

    for r in range(3):
        @pl.when(i % 3 == r)
        def _(r=r):
            wait_gather(r)
            start_gather(idx_next2_ref, (r + 2) % 3)
            mix(bufs[r])

            @pl.when(i == n - 1)
            def _():
                wait_gather((r + 1) % 3)
                wait_gather((r + 2) % 3)


def _peer_experts(xn, res, idx, gates, u_tab, v_tab):
    t, d = xn.shape
    tb = PEER_GATHER_TOKENS
    sub, lanes = PEER_ROW_TILE
    assert t % tb == 0 and d == sub * lanes
    nblk = t // tb
    rows = tb * PEER_PAIRS
    n_exp = u_tab.shape[0]
    uv = jnp.concatenate([u_tab.reshape(n_exp, sub, lanes), v_tab.reshape(n_exp, sub, lanes)], axis=1)
    lane = jnp.arange(d)
    diag = (lane[None, :] % sub == jnp.arange(sub)[:, None]).astype(jnp.float32)
    fold = (lane[:, None] // sub == jnp.arange(PEER_PAIRS)[None, :]).astype(jnp.bfloat16)
    tile_spec = pl.BlockSpec((tb, sub, lanes), lambda i: (i, 0, 0))
    full_spec = lambda a: pl.BlockSpec(a.shape, lambda i: (0, 0))
    ids_spec = lambda ahead: pl.BlockSpec((tb, PEER_PAIRS), lambda i: (jnp.minimum(i + ahead, nblk - 1), 0),
                                          memory_space=pltpu.SMEM)
    slab_buf = pltpu.VMEM((rows, 2 * sub, lanes), jnp.float32)
    out = pl.pallas_call(
        _peer_expert_kernel,
        out_shape=jax.ShapeDtypeStruct((t, sub, lanes), jnp.float32),
        grid=(nblk,),
        in_specs=[ids_spec(0), ids_spec(1), ids_spec(2),
                  tile_spec,
                  pl.BlockSpec((tb, PEER_PAIRS), lambda i: (i, 0)),
                  tile_spec,
                  full_spec(diag), full_spec(fold), full_spec(fold.T),
                  pl.BlockSpec(memory_space=pl.ANY)],
        out_specs=tile_spec,
        scratch_shapes=[slab_buf, slab_buf, slab_buf, pltpu.SemaphoreType.DMA((3,))],
        compiler_params=pltpu.CompilerParams(dimension_semantics=("arbitrary",), vmem_limit_bytes=VMEM_LIMIT),
        name="peer_experts",
    )(idx, idx, idx, xn.reshape(t, sub, lanes), gates, res.reshape(t, sub, lanes), diag, fold, fold.T, uv)
    return out.reshape(t, d)


def peer(h, g, w_q, g_q, sub_keys, u_tab, v_tab):
    xn, idx_t, gates_t = _peer_route(h, g, w_q, g_q, sub_keys)
    return _peer_experts(xn, h, idx_t.T, gates_t.T, u_tab, v_tab)


def kernel(x_prompt, x_sample, p_prompt, p_sample, state_ssm, state_conv, cache_ckv, cache_krope, page_table,
           ln_mix, ln_ffn, ln_ple,
           m_w_in, m_conv_w, m_conv_b, m_dt_bias, m_a_log, m_d, m_g_norm, m_w_out,
           kv_g_in, kv_w_dkv, kv_g_ckv, kv_w_kr, kv_g_kr, kv_w_uk, kv_g_kn, kv_w_uv,
           q_w_dq, q_g_cq, q_w_uq, q_g_qn, q_g_qr, a_w_o,
           peer_w_q, peer_g_q, peer_sub_keys, peer_u, peer_v,
           ple_w_proj, ple_w_gate):
    bf16 = jnp.bfloat16
    bp, lp = x_prompt.shape[:2]
    bs, ls = x_sample.shape[:2]
    n_p, n_s = bp * lp, bs * ls
    pos_p = jnp.arange(lp)
    pos_s = page_table.shape[1] * PAGE_SIZE + jnp.arange(ls)
    h = jnp.concatenate([x_prompt.reshape(n_p, D_MODEL), x_sample.reshape(n_s, D_MODEL)], 0)
    p_all = jnp.concatenate([p_prompt.reshape(DEPTH, n_p, PLE_DIM), p_sample.reshape(DEPTH, n_s, PLE_DIM)], 1)

    def split(a, tail):
        return a[:n_p].reshape((bp, lp) + tail), a[n_p:].reshape((bs, ls) + tail)

    ssm_p, conv_p, ssm_s, conv_s = [], [], [], []
    for i in range(DEPTH):
        if i < N_A:
            w_in = m_w_in[i].astype(bf16)
            z, xbc, dt = _dense(h, [w_in[:, :D_INNER], w_in[:, D_INNER:D_INNER + CONV_DIM],
                                    w_in[:, D_INNER + CONV_DIM:]], gain=ln_mix[i])
            mw = (m_conv_w[i], m_conv_b[i], m_dt_bias[i], m_a_log[i], m_d[i], m_g_norm[i])
            (z_p, z_s), (xbc_p, xbc_s), (dt_p, dt_s) = (split(z, (D_INNER,)), split(xbc, (CONV_DIM,)),
                                                        split(dt, (SSM_HEADS,)))
            conv0 = jnp.zeros((bp, CONV_W - 1, CONV_DIM), h.dtype)
            ssm0 = jnp.zeros((bp, SSM_HEADS, SSM_HEADDIM, SSM_STATE), h.dtype)
            y_p, c_p, s_p = mamba_sequence(z_p, xbc_p, dt_p, conv0, ssm0, *mw)
            y_s, c_s, s_s = mamba_sequence(z_s, xbc_s, dt_s, state_conv[i], state_ssm[i], *mw)
            conv_p.append(c_p)
            ssm_p.append(s_p)
            conv_s.append(c_s)
            ssm_s.append(s_s)
            y = jnp.concatenate([y_p.reshape(n_p, D_INNER), y_s.reshape(n_s, D_INNER)], 0)
            (h,) = _dense(y, [m_w_out[i].astype(bf16)], residual=h)
        else:
            if i == N_A:
                ckv_raw, kr_raw = _dense(h, [kv_w_dkv.astype(bf16), kv_w_kr.astype(bf16)], gain=kv_g_in)
                ckv = rms_norm(ckv_raw, kv_g_ckv)
                ckv_p, ckv_s = split(ckv, (KV_LORA,))
                kr_p, kr_s = split(rms_norm(kr_raw, kv_g_kr), (ROPE_DIM,))
                kr_p, kr_s = rope(kr_p, pos_p), rope(kr_s, pos_s)
                k_raw, v_p = _dense(ckv[:n_p], [kv_w_uk.astype(bf16), kv_w_uv.astype(bf16)])
                kn_p = rms_norm(k_raw.reshape(bp, lp, MLA_HEADS, NOPE_DIM), kv_g_kn)
                v_p = v_p.reshape(bp, lp, MLA_HEADS, V_DIM)
            j = i - N_A
            (cq_raw,) = _dense(h, [q_w_dq[j].astype(bf16)], gain=ln_mix[i])
            (q,) = _dense(cq_raw, [q_w_uq[j].astype(bf16)], gain=q_g_cq[j])
            q = q.reshape(n_p + n_s, MLA_HEADS, NOPE_DIM + ROPE_DIM)
            qn_p, qn_s = split(rms_norm(q[..., :NOPE_DIM], q_g_qn[j]), (MLA_HEADS, NOPE_DIM))
            qr_p, qr_s = split(rms_norm(q[..., NOPE_DIM:], q_g_qr[j]), (MLA_HEADS, ROPE_DIM))
            qr_p, qr_s = rope(qr_p, pos_p), rope(qr_s, pos_s)
            o_p = mla_attend_prompt(qn_p, qr_p, kn_p, kr_p, v_p)
            o_s = mla_attend_sample(qn_s, qr_s, ckv_s, kr_s, cache_ckv, cache_krope, page_table,
                                    kv_w_uk, kv_g_kn, kv_w_uv)
            o = jnp.concatenate([o_p.reshape(n_p, MLA_HEADS * V_DIM), o_s.reshape(n_s, MLA_HEADS * V_DIM)], 0)
            (h,) = _dense(o, [a_w_o[j].astype(bf16)], residual=h)
        h = peer(h, ln_ffn[i], peer_w_q[i], peer_g_q[i], peer_sub_keys[i], peer_u[i], peer_v[i])
        h = _ple(h, p_all[i], ln_ple[i], ple_w_proj[i], ple_w_gate[i])
    hp, hs = split(h, (D_MODEL,))
    return (hp, hs, jnp.stack(ssm_p), jnp.stack(conv_p), ckv_p, kr_p,
            jnp.stack(ssm_s), jnp.stack(conv_s), ckv_s, kr_s)
```

```python
import functools
import math

import jax
import jax.numpy as jnp
from jax import lax
from jax.experimental import pallas as pl
from jax.experimental.pallas import tpu as pltpu

D_MODEL = 1024
DEPTH = 2
PAGE_SIZE = 128
N_A = DEPTH // 2
D_INNER = 2 * D_MODEL
SSM_HEADDIM = 64
SSM_HEADS = D_INNER // SSM_HEADDIM
SSM_GROUPS = 4
SSM_STATE = 128
CONV_W = 4
CONV_DIM = D_INNER + 2 * SSM_GROUPS * SSM_STATE
SSD_CHUNK = 128
MLA_HEADS = 16
Q_LORA = 384
KV_LORA = 256
NOPE_DIM = 64
ROPE_DIM = 32
V_DIM = 64
ROPE_THETA = 10000.0
ATTN_SCALE = (NOPE_DIM + ROPE_DIM) ** -0.5
PEER_HEADS = 8
PEER_KEYS = 128
PEER_QDIM = 256
PEER_TOPK = 16
PLE_DIM = 256
EPS = 1e-6

VMEM_LIMIT = 48 * 1024 * 1024


def rms_norm(x, g):
    xf = x.astype(jnp.float32)
    y = xf * lax.rsqrt(jnp.mean(xf * xf, -1, keepdims=True) + EPS)
    return (y * g.astype(jnp.float32)).astype(x.dtype)


def rope(x, pos):
    half = ROPE_DIM // 2
    inv = ROPE_THETA ** (-jnp.arange(half, dtype=jnp.float32) / half)
    ang = pos.astype(jnp.float32)[:, None] * inv
    ang = ang.reshape((1, ang.shape[0]) + (1,) * (x.ndim - 3) + (half,))
    cos, sin = jnp.cos(ang), jnp.sin(ang)
    xf = x.astype(jnp.float32)
    x1, x2 = xf[..., :half], xf[..., half:]
    return jnp.concatenate([x1 * cos - x2 * sin, x1 * sin + x2 * cos], -1).astype(x.dtype)


DENSE_ROWS = 256


def _dense_kernel(*refs, n_out, has_gain, has_residual):
    refs = list(refs)
    x_ref = refs.pop(0)
    g_ref = refs.pop(0) if has_gain else None
    w_refs = [refs.pop(0) for _ in range(n_out)]
    r_ref = refs.pop(0) if has_residual else None
    x = x_ref[...]
    if has_gain:
        x = x * lax.rsqrt(jnp.mean(x * x, -1, keepdims=True) + EPS) * g_ref[...]
    xb = x.astype(jnp.bfloat16)
    for w_ref, o_ref in zip(w_refs, refs):
        y = jnp.dot(xb, w_ref[...], preferred_element_type=jnp.float32)
        o_ref[...] = y + r_ref[...] if has_residual else y


def _dense(x, weights, gain=None, residual=None):
    rows, k = x.shape
    tm = math.gcd(rows, DENSE_ROWS)
    assert residual is None or len(weights) == 1
    row_spec = lambda n: pl.BlockSpec((tm, n), lambda i: (i, 0))
    full_spec = lambda a: pl.BlockSpec(a.shape, lambda i: (0, 0))
    args, specs = [x], [row_spec(k)]
    if gain is not None:
        args.append(gain.reshape(1, k).astype(jnp.float32))
        specs.append(full_spec(args[-1]))
    for w in weights:
        args.append(w)
        specs.append(full_spec(w))
    if residual is not None:
        args.append(residual)
        specs.append(row_spec(residual.shape[1]))
    return pl.pallas_call(
        functools.partial(_dense_kernel, n_out=len(weights), has_gain=gain is not None,
                          has_residual=residual is not None),
        out_shape=tuple(jax.ShapeDtypeStruct((rows, w.shape[1]), jnp.float32) for w in weights),
        grid=(rows // tm,),
        in_specs=specs,
        out_specs=tuple(row_spec(w.shape[1]) for w in weights),
        compiler_params=pltpu.CompilerParams(dimension_semantics=("arbitrary",), vmem_limit_bytes=VMEM_LIMIT),
        name="dense",
    )(*args)


def _ple_kernel(h_ref, p_ref, g_ref, wp_ref, wg_ref, o_ref):
    f32, bf16 = jnp.float32, jnp.bfloat16
    h = h_ref[...]
    hn = h * lax.rsqrt(jnp.mean(h * h, -1, keepdims=True) + EPS) * g_ref[...]
    gate = jax.nn.sigmoid(jnp.dot(hn.astype(bf16), wg_ref[...], preferred_element_type=f32))
    proj = jnp.dot(p_ref[...].astype(bf16), wp_ref[...], preferred_element_type=f32)
    o_ref[...] = h + proj * gate


def _ple(h, p, g, w_proj, w_gate):
    rows, d = h.shape
    tm = math.gcd(rows, DENSE_ROWS)
    row_spec = lambda n: pl.BlockSpec((tm, n), lambda i: (i, 0))
    full_spec = lambda a: pl.BlockSpec(a.shape, lambda i: (0, 0))
    g2 = g.reshape(1, d).astype(jnp.float32)
    wp, wg = w_proj.astype(jnp.bfloat16), w_gate.astype(jnp.bfloat16)
    return pl.pallas_call(
        _ple_kernel,
        out_shape=jax.ShapeDtypeStruct((rows, d), jnp.float32),
        grid=(rows // tm,),
        in_specs=[row_spec(d), row_spec(p.shape[1]), full_spec(g2), full_spec(wp), full_spec(wg)],
        out_specs=row_spec(d),
        compiler_params=pltpu.CompilerParams(dimension_semantics=("arbitrary",), vmem_limit_bytes=VMEM_LIMIT),
        name="ple",
    )(h, p, g2, wp, wg)


SSD_TAIL = 8


def _ssd_kernel(z_ref, xbc_ref, dt_ref, prev_ref, h0_ref, cw_ref, cb_ref, dtb_ref, a_ref, dskip_ref, gn_ref,
                expand_ref, expand_t_ref, y_ref, hout_ref, ext_scr, h_scr, *, n_valid):
    f32, bf16 = jnp.float32, jnp.bfloat16
    hi = lax.Precision.HIGHEST
    q = z_ref.shape[0]
    c = pl.program_id(1)
    heads_per_group = SSM_HEADS // SSM_GROUPS
    gw = heads_per_group * SSM_HEADDIM
    nt = (((1,), (1,)), ((), ()))
    tn = (((0,), (0,)), ((), ()))

    @pl.when(c == 0)
    def _():
        ext_scr[0:SSD_TAIL, :] = prev_ref[...]
        h_scr[...] = h0_ref[...]

    ext_scr[SSD_TAIL:SSD_TAIL + q, :] = xbc_ref[...]
    conv = cb_ref[...]
    for w in range(CONV_W):
        conv = conv + ext_scr[pl.ds(SSD_TAIL - (CONV_W - 1) + w, q), :] * cw_ref[w:w + 1, :]
    tail = ext_scr[q:q + SSD_TAIL, :]
    ext_scr[0:SSD_TAIL, :] = tail
    act = conv * jax.nn.sigmoid(conv)
    xs = act[:, :D_INNER]
    bmat = act[:, D_INNER:D_INNER + SSM_GROUPS * SSM_STATE].astype(bf16)
    cmat = act[:, D_INNER + SSM_GROUPS * SSM_STATE:].astype(bf16)

    row = lax.broadcasted_iota(jnp.int32, (q, q), 0)
    col = lax.broadcasted_iota(jnp.int32, (q, q), 1)
    dt = jax.nn.softplus(dt_ref[...] + dtb_ref[...])
    if n_valid < q:
        dt = jnp.where(lax.broadcasted_iota(jnp.int32, dt.shape, 0) < n_valid, dt, 0.0)
    a = dt * a_ref[...]
    a_cum = jnp.dot((col <= row).astype(f32), a, precision=hi, preferred_element_type=f32)
    a_cum_t = jnp.dot(a.T, (row <= col).astype(f32), precision=hi, preferred_element_type=f32)
    a_last = a_cum[q - 1:q, :]
    expand = expand_ref[...]
    widen = lambda v: jnp.dot(v, expand, precision=hi, preferred_element_type=f32)
    xd = xs * widen(dt)
    xd_b = xd.astype(bf16)
    xdd_b = (xd * widen(jnp.exp(a_last - a_cum))).astype(bf16)
    grow = widen(jnp.exp(a_cum))
    carry = jnp.dot(expand_t_ref[...], jnp.exp(a_cum_t[:, q - 1:q]), precision=hi,
                    preferred_element_type=f32)

    y_parts = []
    for g in range(SSM_GROUPS):
        bg = bmat[:, g * SSM_STATE:(g + 1) * SSM_STATE]
        cg = cmat[:, g * SSM_STATE:(g + 1) * SSM_STATE]
        cbg = lax.dot_general(cg, bg, nt, preferred_element_type=f32)
        h_in = h_scr[g * gw:(g + 1) * gw, :]
        y_off = lax.dot_general(cg, h_in.astype(bf16), nt, preferred_element_type=f32)
        y_g = y_off * grow[:, g * gw:(g + 1) * gw]
        diag_parts = []
        for r in range(heads_per_group):
            hd = g * heads_per_group + r
            seg = a_cum[:, hd:hd + 1] - a_cum_t[hd:hd + 1, :]
            lmat = jnp.exp(jnp.where(col <= row, seg, -jnp.inf))
            lanes = slice(hd * SSM_HEADDIM, (hd + 1) * SSM_HEADDIM)
            diag_parts.append(jnp.dot((cbg * lmat).astype(bf16), xd_b[:, lanes], preferred_element_type=f32))
        y_parts.append(y_g + jnp.concatenate(diag_parts, 1))
        st = lax.dot_general(xdd_b[:, g * gw:(g + 1) * gw], bg, tn, preferred_element_type=f32)
        h_scr[g * gw:(g + 1) * gw, :] = h_in * carry[g * gw:(g + 1) * gw, :] + st
    y = jnp.concatenate(y_parts, 1) + dskip_ref[...] * xs
    zz = z_ref[...]
    y = y * (zz * jax.nn.sigmoid(zz))
    gsz = D_INNER // SSM_GROUPS
    outs = []
    for g in range(SSM_GROUPS):
        seg = y[:, g * gsz:(g + 1) * gsz]
        outs.append(seg * lax.rsqrt(jnp.mean(seg * seg, -1, keepdims=True) + EPS))
    y_ref[...] = jnp.concatenate(outs, 1) * gn_ref[...]

    @pl.when(c == pl.num_programs(1) - 1)
    def _():
        hout_ref[...] = h_scr[...]


def _ssd(z, xbc, dt, conv_prev, h0, conv_w, conv_b, dt_bias, a_log, d_skip, g_norm, chunk, n_valid):
    b, l, _ = z.shape
    f32 = jnp.float32
    assert l % chunk == 0
    prev = jnp.pad(conv_prev.astype(f32), ((0, 0), (SSD_TAIL - (CONV_W - 1), 0), (0, 0)))
    rows = SSM_HEADS * SSM_HEADDIM
    expand = jnp.repeat(jnp.eye(SSM_HEADS, dtype=f32), SSM_HEADDIM, axis=1)
    vec = lambda v, n: v.astype(f32).reshape(1, n)
    seq_spec = lambda n: pl.BlockSpec((None, chunk, n), lambda bi, ci: (bi, ci, 0))
    bat_spec = lambda r, n: pl.BlockSpec((None, r, n), lambda bi, ci: (bi, 0, 0))
    full_spec = lambda a: pl.BlockSpec(a.shape, lambda bi, ci: (0, 0))
    params = [conv_w.astype(f32), vec(conv_b, CONV_DIM), vec(dt_bias, SSM_HEADS), vec(-jnp.exp(a_log.astype(f32)), SSM_HEADS),
              vec(jnp.repeat(d_skip, SSM_HEADDIM), D_INNER), vec(g_norm, D_INNER), expand, expand.T]
    y, h_new = pl.pallas_call(
        functools.partial(_ssd_kernel, n_valid=n_valid),
        out_shape=(jax.ShapeDtypeStruct((b, l, D_INNER), f32), jax.ShapeDtypeStruct((b, rows, SSM_STATE), f32)),
        grid=(b, l // chunk),
        in_specs=[seq_spec(D_INNER), seq_spec(CONV_DIM), seq_spec(SSM_HEADS), bat_spec(SSD_TAIL, CONV_DIM),
                  bat_spec(rows, SSM_STATE)] + [full_spec(p) for p in params],
        out_specs=(seq_spec(D_INNER), bat_spec(rows, SSM_STATE)),
        scratch_shapes=[pltpu.VMEM((chunk + SSD_TAIL, CONV_DIM), f32), pltpu.VMEM((rows, SSM_STATE), f32)],
        compiler_params=pltpu.CompilerParams(dimension_semantics=("arbitrary", "arbitrary"),
                                             vmem_limit_bytes=VMEM_LIMIT),
        name="ssd",
    )(z, xbc, dt, prev, h0.astype(f32).reshape(b, rows, SSM_STATE), *params)
    return y, h_new.reshape(b, SSM_HEADS, SSM_HEADDIM, SSM_STATE)


SSD_PAD_ROWS = 16


def mamba_sequence(z, xbc, dt, conv_prev, h0, conv_w, conv_b, dt_bias, a_log, d_skip, g_norm):
    b, l, _ = z.shape
    conv_new = jnp.concatenate([conv_prev.astype(xbc.dtype), xbc], 1)[:, -(CONV_W - 1):]
    mw = (conv_w, conv_b, dt_bias, a_log, d_skip, g_norm)
    if l % SSD_CHUNK == 0:
        y, h_new = _ssd(z, xbc, dt, conv_prev, h0, *mw, chunk=SSD_CHUNK, n_valid=SSD_CHUNK)
    else:
        assert l <= SSD_PAD_ROWS
        pad = lambda a: jnp.pad(a, ((0, 0), (0, SSD_PAD_ROWS - l), (0, 0)))
        y, h_new = _ssd(pad(z), pad(xbc), pad(dt), conv_prev, h0, *mw, chunk=SSD_PAD_ROWS, n_valid=l)
        y = y[:, :l]
    return y, conv_new, h_new.astype(z.dtype)


FLASH_TILE = 1024
FLASH_LANES = 128


def _flash_kernel(q_ref, k_ref, vt_ref, o_ref, m_scr, l_scr, acc_scr):
    f32 = jnp.float32
    qi, ki = pl.program_id(2), pl.program_id(3)

    @pl.when(ki == 0)
    def _():
        m_scr[...] = jnp.full(m_scr.shape, -jnp.inf, f32)
        l_scr[...] = jnp.zeros(l_scr.shape, f32)
        acc_scr[...] = jnp.zeros(acc_scr.shape, f32)

    def update(on_diagonal):
        s = lax.dot_general(k_ref[...], q_ref[...], (((1,), (1,)), ((), ())), preferred_element_type=f32)
        s = s * ATTN_SCALE
        if on_diagonal:
            key = lax.broadcasted_iota(jnp.int32, s.shape, 0)
            query = lax.broadcasted_iota(jnp.int32, s.shape, 1)
            s = jnp.where(key <= query, s, -jnp.inf)
        m_prev = m_scr[...]
        m_new = jnp.maximum(m_prev, jnp.max(s, axis=0, keepdims=True))
        alpha = jnp.exp(m_prev - m_new)
        p = jnp.exp(s - m_new)
        l_scr[...] = alpha * l_scr[...] + jnp.sum(p, axis=0, keepdims=True)
        acc_scr[...] = alpha * acc_scr[...] + jnp.dot(vt_ref[...], p.astype(vt_ref.dtype),
                                                      preferred_element_type=f32)
        m_scr[...] = m_new

    @pl.when(ki < qi)
    def _():
        update(False)

    @pl.when(ki == qi)
    def _():
        update(True)
        o_ref[...] = acc_scr[...] / l_scr[...]


def _flash_attention(q, k, vt):
    b, h, l, _ = q.shape
    t = min(FLASH_TILE, l)
    assert l % t == 0
    nblk = l // t
    return pl.pallas_call(
        _flash_kernel,
        out_shape=jax.ShapeDtypeStruct((b, h, V_DIM, l), jnp.float32),
        grid=(b, h, nblk, nblk),
        in_specs=[pl.BlockSpec((None, None, t, FLASH_LANES), lambda bi, hi, qi, ki: (bi, hi, qi, 0)),
                  pl.BlockSpec((None, None, t, FLASH_LANES),
                               lambda bi, hi, qi, ki: (bi, hi, jnp.minimum(ki, qi), 0)),
                  pl.BlockSpec((None, None, V_DIM, t), lambda bi, hi, qi, ki: (bi, hi, 0, jnp.minimum(ki, qi)))],
        out_specs=pl.BlockSpec((None, None, V_DIM, t), lambda bi, hi, qi, ki: (bi, hi, 0, qi)),
        scratch_shapes=[pltpu.VMEM((1, t), jnp.float32), pltpu.VMEM((1, t), jnp.float32),
                        pltpu.VMEM((V_DIM, t), jnp.float32)],
        compiler_params=pltpu.CompilerParams(
            dimension_semantics=("arbitrary", "arbitrary", "arbitrary", "arbitrary"),
            vmem_limit_bytes=VMEM_LIMIT),
        name="prompt_attention",
    )(q, k, vt)


def mla_attend_prompt(qn, qr, kn, kr, v):
    b, l = qn.shape[:2]
    bf16 = jnp.bfloat16
    pad = jnp.zeros((b, l, MLA_HEADS, FLASH_LANES - NOPE_DIM - ROPE_DIM), bf16)
    q_cat = jnp.concatenate([qn.astype(bf16), qr.astype(bf16), pad], -1).transpose(0, 2, 1, 3)
    kr_h = jnp.broadcast_to(kr.astype(bf16)[:, :, None, :], (b, l, MLA_HEADS, ROPE_DIM))
    k_cat = jnp.concatenate([kn.astype(bf16), kr_h, pad], -1).transpose(0, 2, 1, 3)
    o = _flash_attention(q_cat, k_cat, v.astype(bf16).transpose(0, 2, 3, 1))
    return o.transpose(0, 3, 1, 2).reshape(b, l, MLA_HEADS * V_DIM)


DECODE_PAGES = 8
DECODE_NEW_ROWS = 8


def _decode_kernel(pt_ref, *refs):
    f32, bf16 = jnp.float32, jnp.bfloat16
    npg = DECODE_PAGES
    c_refs, kr_refs = refs[:npg], refs[npg:2 * npg]
    (qn_ref, qr_ref, cnew_ref, krnew_ref, wuk_ref, ind_ref, o_ref, m_scr, l_scr, acc_scr) = refs[2 * npg:]
    j = pl.program_id(1)
    nt = (((1,), (1,)), ((), ()))
    rows = qn_ref.shape[0]

    @pl.when(j == 0)
    def _():
        m_scr[...] = jnp.full(m_scr.shape, -jnp.inf, f32)
        l_scr[...] = jnp.zeros(l_scr.shape, f32)
        acc_scr[...] = jnp.zeros(acc_scr.shape, f32)

    def scores(c, kr):
        cb = c.astype(bf16)
        k = jnp.dot(cb, wuk_ref[...], preferred_element_type=f32)
        ms = lax.dot_general(ind_ref[...], (k * k).astype(bf16), nt, preferred_element_type=f32)
        r = lax.rsqrt(ms[:rows] * (1.0 / NOPE_DIM) + EPS)
        s = r * lax.dot_general(qn_ref[...], k.astype(bf16), nt, preferred_element_type=f32)
        s = s + lax.dot_general(qr_ref[...], kr.astype(bf16), nt, preferred_element_type=f32)
        return s * ATTN_SCALE, cb

    def update(s, cb):
        m_prev = m_scr[...]
        m_new = jnp.maximum(m_prev, jnp.max(s, axis=-1, keepdims=True))
        alpha = jnp.exp(m_prev - m_new)
        p = jnp.exp(s - m_new)
        l_scr[...] = alpha * l_scr[...] + jnp.sum(p, axis=-1, keepdims=True)
        acc_scr[...] = alpha * acc_scr[...] + jnp.dot(p.astype(bf16), cb, preferred_element_type=f32)
        m_scr[...] = m_new

    c_all = jnp.concatenate([r[...] for r in c_refs], 0)
    kr_all = jnp.concatenate([r[...] for r in kr_refs], 0)
    update(*scores(c_all, kr_all))

    @pl.when(j == pl.num_programs(1) - 1)
    def _():
        s, cb = scores(cnew_ref[...], krnew_ref[...])
        key = lax.broadcasted_iota(jnp.int32, s.shape, 1)
        query = lax.broadcasted_iota(jnp.int32, s.shape, 0) // MLA_HEADS
        update(jnp.where(key <= query, s, -jnp.inf), cb)
        o_ref[...] = acc_scr[...] / l_scr[...]


def mla_attend_sample(qn, qr, ckv_new, kr_new, cache_ckv, cache_krope, page_table, w_uk, g_kn, w_uv):
    b, lq = qn.shape[:2]
    f32, bf16 = jnp.float32, jnp.bfloat16
    n_pages = page_table.shape[1]
    npg = DECODE_PAGES
    assert n_pages % npg == 0 and lq <= DECODE_NEW_ROWS
    rows = lq * MLA_HEADS
    hd = MLA_HEADS * NOPE_DIM
    eye = jnp.eye(MLA_HEADS, dtype=f32)
    qg = qn.astype(f32) * g_kn.astype(f32)
    qn_bd = (qg[:, :, :, None, :] * eye[None, None, :, :, None]).reshape(b, rows, hd).astype(bf16)
    qr_rows = qr.reshape(b, rows, ROPE_DIM).astype(bf16)
    ind = jnp.tile(jnp.repeat(eye, NOPE_DIM, axis=1), (lq, 1))
    ind = jnp.pad(ind, ((0, -rows % 128), (0, 0))).astype(bf16)
    pad_new = ((0, 0), (0, DECODE_NEW_ROWS - lq), (0, 0))
    cnew = jnp.pad(ckv_new.astype(f32), pad_new)
    krnew = jnp.pad(kr_new.astype(f32), pad_new)

    def page_spec(width, i):
        return pl.BlockSpec((None, PAGE_SIZE, width), lambda bi, j, pt: (pt[bi, j * npg + i], 0, 0))

    def batch_spec(shape):
        return pl.BlockSpec((None,) + shape, lambda bi, j, pt: (bi, 0, 0))

    grid_spec = pltpu.PrefetchScalarGridSpec(
        num_scalar_prefetch=1,
        grid=(b, n_pages // npg),
        in_specs=([page_spec(KV_LORA, i) for i in range(npg)] + [page_spec(ROPE_DIM, i) for i in range(npg)] +
                  [batch_spec((rows, hd)), batch_spec((rows, ROPE_DIM)),
                   batch_spec((DECODE_NEW_ROWS, KV_LORA)), batch_spec((DECODE_NEW_ROWS, ROPE_DIM)),
                   pl.BlockSpec((KV_LORA, hd), lambda bi, j, pt: (0, 0)),
                   pl.BlockSpec(ind.shape, lambda bi, j, pt: (0, 0))]),
        out_specs=batch_spec((rows, KV_LORA)),
        scratch_shapes=[pltpu.VMEM((rows, 1), f32), pltpu.VMEM((rows, 1), f32), pltpu.VMEM((rows, KV_LORA), f32)],
    )
    o_lat = pl.pallas_call(
        _decode_kernel,
        grid_spec=grid_spec,
        out_shape=jax.ShapeDtypeStruct((b, rows, KV_LORA), f32),
        compiler_params=pltpu.CompilerParams(dimension_semantics=("arbitrary", "arbitrary"),
                                             vmem_limit_bytes=VMEM_LIMIT),
        name="sample_attention",
    )(page_table, *([cache_ckv] * npg), *([cache_krope] * npg), qn_bd, qr_rows, cnew, krnew,
      w_uk.astype(bf16), ind)
    o_lat = o_lat.reshape(b, lq, MLA_HEADS, KV_LORA)
    o = jnp.einsum('bqhc,chd->bqhd', o_lat, w_uv.astype(f32).reshape(KV_LORA, MLA_HEADS, V_DIM))
    return o.reshape(b, lq, MLA_HEADS * V_DIM).astype(qn.dtype)


PEER_ROUTE_TOKENS = 256
PEER_LANE_TILE = 128
PEER_GATHER_TOKENS = 16
PEER_ROW_TILE = (8, 128)
PEER_PAIRS = PEER_HEADS * PEER_TOPK
PEER_DMA_QUEUES = 2


def _topk_rows(s, k, ids=None):
    if ids is None:
        ids = lax.broadcasted_iota(jnp.int32, s.shape, 0).astype(jnp.float32)
    vals, idxs = [], []
    for _ in range(k):
        m = jnp.max(s, axis=0, keepdims=True)
        ix = jnp.min(jnp.where(s == m, ids, jnp.inf), axis=0, keepdims=True)
        vals.append(m)
        idxs.append(ix)
        s = jnp.where(ids == ix, -jnp.inf, s)
    return jnp.concatenate(vals, 0), jnp.concatenate(idxs, 0)


def _pair_candidates(v1, v2):
    sub = 8
    assert PEER_TOPK == 2 * sub
    row = lax.broadcasted_iota(jnp.int32, (sub, v1.shape[1]), 0).astype(jnp.float32)
    lo, hi = v2[:sub], v2[sub:]
    vals = [v1[0:1] + lo, v1[0:1] + hi] + [v1[a:a + 1] + lo for a in range(1, sub)] + [v1[sub:] + v2[0:1]]
    ids = [row, row + sub] + [row + a * PEER_TOPK for a in range(1, sub)] + [(row + sub) * PEER_TOPK]
    return jnp.concatenate(vals, 0), jnp.concatenate(ids, 0)


def _select_rows(table, sel):
    out = jnp.zeros(sel.shape, table.dtype)
    for j in range(table.shape[0]):
        out = jnp.where(sel == float(j), table[j:j + 1, :], out)
    return out


def _peer_route_kernel(x_ref, g_ref, wq_ref, gq_ref, k1_ref, k2_ref, xn_ref, idx_ref, gate_ref, xn_scr):
    f32, bf16 = jnp.float32, jnp.bfloat16

    @pl.when(pl.program_id(1) == 0)
    def _():
        x = x_ref[...]
        y = x * lax.rsqrt(jnp.mean(x * x, -1, keepdims=True) + EPS) * g_ref[...]
        xn_ref[...] = y
        xn_scr[...] = y.astype(bf16)

    q = jnp.dot(xn_scr[...], wq_ref[...], preferred_element_type=f32)
    qn = (q * lax.rsqrt(jnp.mean(q * q, -1, keepdims=True) + EPS) * gq_ref[...]).astype(bf16)
    half = PEER_QDIM // 2
    nt = (((1,), (1,)), ((), ()))
    s1 = lax.dot_general(k1_ref[...], qn[:, :half], nt, preferred_element_type=f32)
    s2 = lax.dot_general(k2_ref[...], qn[:, half:], nt, preferred_element_type=f32)
    for c in range(s1.shape[1] // PEER_LANE_TILE):
        cs = slice(c * PEER_LANE_TILE, (c + 1) * PEER_LANE_TILE)
        v1, i1 = _topk_rows(s1[:, cs], PEER_TOPK)
        v2, i2 = _topk_rows(s2[:, cs], PEER_TOPK)
        cand, cand_ids = _pair_candidates(v1, v2)
        vals, ci = _topk_rows(cand, PEER_TOPK, cand_ids)
        a = jnp.floor(ci * (1.0 / PEER_TOPK))
        b = ci - a * PEER_TOPK
        e = _select_rows(i1, a) * float(PEER_KEYS) + _select_rows(i2, b)
        ex = jnp.exp(vals - vals[0:1, :])
        idx_ref[:, cs] = e.astype(jnp.int32)
        gate_ref[:, cs] = ex / jnp.sum(ex, axis=0, keepdims=True)


def _peer_route(x, g, w_q, g_q, sub_keys):
    t, d = x.shape
    tb = PEER_ROUTE_TOKENS
    assert t % tb == 0
    bf16 = jnp.bfloat16
    half = PEER_QDIM // 2
    return pl.pallas_call(
        _peer_route_kernel,
        out_shape=(jax.ShapeDtypeStruct((t, d), jnp.float32),
                   jax.ShapeDtypeStruct((PEER_PAIRS, t), jnp.int32),
                   jax.ShapeDtypeStruct((PEER_PAIRS, t), jnp.float32)),
        grid=(t // tb, PEER_HEADS),
        in_specs=[pl.BlockSpec((tb, d), lambda i, h: (i, 0)),
                  pl.BlockSpec((1, d), lambda i, h: (0, 0)),
                  pl.BlockSpec((d, PEER_QDIM), lambda i, h: (0, h)),
                  pl.BlockSpec((1, PEER_QDIM), lambda i, h: (0, 0)),
                  pl.BlockSpec((PEER_KEYS, half), lambda i, h: (0, 0)),
                  pl.BlockSpec((PEER_KEYS, half), lambda i, h: (0, 0))],
        out_specs=(pl.BlockSpec((tb, d), lambda i, h: (i, 0)),
                   pl.BlockSpec((PEER_TOPK, tb), lambda i, h: (h, i)),
                   pl.BlockSpec((PEER_TOPK, tb), lambda i, h: (h, i))),
        scratch_shapes=[pltpu.VMEM((tb, d), bf16)],
        compiler_params=pltpu.CompilerParams(dimension_semantics=("arbitrary", "arbitrary")),
        name="peer_route",
    )(x, g.reshape(1, d), w_q.astype(bf16), g_q.reshape(1, PEER_QDIM),
      sub_keys[0].astype(bf16), sub_keys[1].astype(bf16))


def _peer_expert_kernel(idx_ref, idx_next_ref, idx_next2_ref, x_ref, gate_ref, res_ref, diag_ref, fold_ref,
                        spread_ref, uv_hbm, out_ref, buf0, buf1, buf2, sems):
    f32, bf16 = jnp.float32, jnp.bfloat16
    tb = PEER_GATHER_TOKENS
    sub = PEER_ROW_TILE[0]
    rows = tb * PEER_PAIRS
    i = pl.program_id(0)
    n = pl.num_programs(0)
    bufs = (buf0, buf1, buf2)

    def start_gather(ids_ref, j):
        for t in range(tb):
            for k in range(PEER_PAIRS):
                pltpu.make_async_copy(uv_hbm.at[ids_ref[t, k]], bufs[j].at[t * PEER_PAIRS + k], sems.at[j]).start(
                    priority=k % PEER_DMA_QUEUES)

    def wait_gather(j):
        pltpu.make_async_copy(uv_hbm.at[pl.ds(0, rows)], bufs[j], sems.at[j]).wait()

    def mix(cur):
        nt = (((1,), (1,)), ((), ()))
        diag = diag_ref[...]
        flat = (PEER_PAIRS * sub, PEER_ROW_TILE[1])

        def slab_rows(t, high):
            words = cur[t * PEER_PAIRS:(t + 1) * PEER_PAIRS].reshape(flat)
            bits = words & jnp.uint32(0xFFFF0000) if high else words << 16
            return lax.bitcast_convert_type(bits, f32).astype(bf16)

        e = jnp.concatenate([lax.dot_general(x_ref[t].astype(bf16), slab_rows(t, False), nt,
                                             preferred_element_type=f32) * diag for t in range(tb)], 0)
        e_hi = e.astype(bf16)
        e_lo = (e - e_hi.astype(f32)).astype(bf16)
        h = jnp.dot(jnp.concatenate([e_hi, e_lo], 0), fold_ref[...], preferred_element_type=f32)
        h = h[:tb * sub] + h[tb * sub:]
        h = jnp.sum(h.reshape(tb, sub, PEER_PAIRS), axis=1)
        act = 0.5 * h * (1.0 + lax.erf(h * math.sqrt(0.5)))
        w = (gate_ref[...] * act).astype(bf16)
        wrep = jnp.broadcast_to(w[:, None, :], (tb, sub, PEER_PAIRS)).reshape(tb * sub, PEER_PAIRS)
        wfull = jnp.dot(wrep, spread_ref[...], preferred_element_type=f32)
        for t in range(tb):
            wm = (wfull[t * sub:(t + 1) * sub] * diag).astype(bf16)
            out_ref[t] = res_ref[t] + jnp.dot(wm, slab_rows(t, True), preferred_element_type=f32)

    @pl.when(i == 0)
    def _():
        start_gather(idx_ref, 0)
        start_gather(idx_next_ref, 1)

    for r in range(3):
        @pl.when(i % 3 == r)
        def _(r=r):
            wait_gather(r)
            start_gather(idx_next2_ref, (r + 2) % 3)
            mix(bufs[r])

            @pl.when(i == n - 1)
            def _():
                wait_gather((r + 1) % 3)
                wait_gather((r + 2) % 3)


def _peer_experts(xn, res, idx, gates, u_tab, v_tab):
    t, d = xn.shape
    tb = PEER_GATHER_TOKENS
    sub, lanes = PEER_ROW_TILE
    assert t % tb == 0 and d == sub * lanes
    nblk = t // tb
    rows = tb * PEER_PAIRS
    n_exp = u_tab.shape[0]
    halves = lambda tab: lax.bitcast_convert_type(tab.astype(jnp.bfloat16), jnp.uint16).astype(jnp.uint32)
    uv = ((halves(v_tab) << 16) | halves(u_tab)).reshape(n_exp, sub, lanes)
    lane = jnp.arange(d)
    diag = (lane[None, :] % sub == jnp.arange(sub)[:, None]).astype(jnp.float32)
    fold = (lane[:, None] // sub == jnp.arange(PEER_PAIRS)[None, :]).astype(jnp.bfloat16)
    tile_spec = pl.BlockSpec((tb, sub, lanes), lambda i: (i, 0, 0))
    full_spec = lambda a: pl.BlockSpec(a.shape, lambda i: (0, 0))
    ids_spec = lambda ahead: pl.BlockSpec((tb, PEER_PAIRS), lambda i: (jnp.minimum(i + ahead, nblk - 1), 0),
                                          memory_space=pltpu.SMEM)
    slab_buf = pltpu.VMEM((rows, sub, lanes), jnp.uint32)
    out = pl.pallas_call(
        _peer_expert_kernel,
        out_shape=jax.ShapeDtypeStruct((t, sub, lanes), jnp.float32),
        grid=(nblk,),
        in_specs=[ids_spec(0), ids_spec(1), ids_spec(2),
                  tile_spec,
                  pl.BlockSpec((tb, PEER_PAIRS), lambda i: (i, 0)),
                  tile_spec,
                  full_spec(diag), full_spec(fold), full_spec(fold.T),
                  pl.BlockSpec(memory_space=pl.ANY)],
        out_specs=tile_spec,
        scratch_shapes=[slab_buf, slab_buf, slab_buf, pltpu.SemaphoreType.DMA((3,))],
        compiler_params=pltpu.CompilerParams(dimension_semantics=("arbitrary",), vmem_limit_bytes=VMEM_LIMIT),
        name="peer_experts",
    )(idx, idx, idx, xn.reshape(t, sub, lanes), gates, res.reshape(t, sub, lanes), diag, fold, fold.T, uv)
    return out.reshape(t, d)


def peer(h, g, w_q, g_q, sub_keys, u_tab, v_tab):
    xn, idx_t, gates_t = _peer_route(h, g, w_q, g_q, sub_keys)
    return _peer_experts(xn, h, idx_t.T, gates_t.T, u_tab, v_tab)


def kernel(x_prompt, x_sample, p_prompt, p_sample, state_ssm, state_conv, cache_ckv, cache_krope, page_table,
           ln_mix, ln_ffn, ln_ple,
           m_w_in, m_conv_w, m_conv_b, m_dt_bias, m_a_log, m_d, m_g_norm, m_w_out,
           kv_g_in, kv_w_dkv, kv_g_ckv, kv_w_kr, kv_g_kr, kv_w_uk, kv_g_kn, kv_w_uv,
           q_w_dq, q_g_cq, q_w_uq, q_g_qn, q_g_qr, a_w_o,
           peer_w_q, peer_g_q, peer_sub_keys, peer_u, peer_v,
           ple_w_proj, ple_w_gate):
    bf16 = jnp.bfloat16
    bp, lp = x_prompt.shape[:2]
    bs, ls = x_sample.shape[:2]
    n_p, n_s = bp * lp, bs * ls
    pos_p = jnp.arange(lp)
    pos_s = page_table.shape[1] * PAGE_SIZE + jnp.arange(ls)
    h = jnp.concatenate([x_prompt.reshape(n_p, D_MODEL), x_sample.reshape(n_s, D_MODEL)], 0)
    p_all = jnp.concatenate([p_prompt.reshape(DEPTH, n_p, PLE_DIM), p_sample.reshape(DEPTH, n_s, PLE_DIM)], 1)

    def split(a, tail):
        return a[:n_p].reshape((bp, lp) + tail), a[n_p:].reshape((bs, ls) + tail)

    ssm_p, conv_p, ssm_s, conv_s = [], [], [], []
    for i in range(DEPTH):
        if i < N_A:
            w_in = m_w_in[i].astype(bf16)
            z, xbc, dt = _dense(h, [w_in[:, :D_INNER], w_in[:, D_INNER:D_INNER + CONV_DIM],
                                    w_in[:, D_INNER + CONV_DIM:]], gain=ln_mix[i])
            mw = (m_conv_w[i], m_conv_b[i], m_dt_bias[i], m_a_log[i], m_d[i], m_g_norm[i])
            (z_p, z_s), (xbc_p, xbc_s), (dt_p, dt_s) = (split(z, (D_INNER,)), split(xbc, (CONV_DIM,)),
                                                        split(dt, (SSM_HEADS,)))
            conv0 = jnp.zeros((bp, CONV_W - 1, CONV_DIM), h.dtype)
            ssm0 = jnp.zeros((bp, SSM_HEADS, SSM_HEADDIM, SSM_STATE), h.dtype)
            y_p, c_p, s_p = mamba_sequence(z_p, xbc_p, dt_p, conv0, ssm0, *mw)
            y_s, c_s, s_s = mamba_sequence(z_s, xbc_s, dt_s, state_conv[i], state_ssm[i], *mw)
            conv_p.append(c_p)
            ssm_p.append(s_p)
            conv_s.append(c_s)
            ssm_s.append(s_s)
            y = jnp.concatenate([y_p.reshape(n_p, D_INNER), y_s.reshape(n_s, D_INNER)], 0)
            (h,) = _dense(y, [m_w_out[i].astype(bf16)], residual=h)
        else:
            if i == N_A:
                ckv_raw, kr_raw = _dense(h, [kv_w_dkv.astype(bf16), kv_w_kr.astype(bf16)], gain=kv_g_in)
                ckv = rms_norm(ckv_raw, kv_g_ckv)
                ckv_p, ckv_s = split(ckv, (KV_LORA,))
                kr_p, kr_s = split(rms_norm(kr_raw, kv_g_kr), (ROPE_DIM,))
                kr_p, kr_s = rope(kr_p, pos_p), rope(kr_s, pos_s)
                k_raw, v_p = _dense(ckv[:n_p], [kv_w_uk.astype(bf16), kv_w_uv.astype(bf16)])
                kn_p = rms_norm(k_raw.reshape(bp, lp, MLA_HEADS, NOPE_DIM), kv_g_kn)
                v_p = v_p.reshape(bp, lp, MLA_HEADS, V_DIM)
            j = i - N_A
            (cq_raw,) = _dense(h, [q_w_dq[j].astype(bf16)], gain=ln_mix[i])
            (q,) = _dense(cq_raw, [q_w_uq[j].astype(bf16)], gain=q_g_cq[j])
            q = q.reshape(n_p + n_s, MLA_HEADS, NOPE_DIM + ROPE_DIM)
            qn_p, qn_s = split(rms_norm(q[..., :NOPE_DIM], q_g_qn[j]), (MLA_HEADS, NOPE_DIM))
            qr_p, qr_s = split(rms_norm(q[..., NOPE_DIM:], q_g_qr[j]), (MLA_HEADS, ROPE_DIM))
            qr_p, qr_s = rope(qr_p, pos_p), rope(qr_s, pos_s)
            o_p = mla_attend_prompt(qn_p, qr_p, kn_p, kr_p, v_p)
            o_s = mla_attend_sample(qn_s, qr_s, ckv_s, kr_s, cache_ckv, cache_krope, page_table,
                                    kv_w_uk, kv_g_kn, kv_w_uv)
            o = jnp.concatenate([o_p.reshape(n_p, MLA_HEADS * V_DIM), o_s.reshape(n_s, MLA_HEADS * V_DIM)], 0)
            (h,) = _dense(o, [a_w_o[j].astype(bf16)], residual=h)
        h = peer(h, ln_ffn[i], peer_w_q[i], peer_g_q[i], peer_sub_keys[i], peer_u[i], peer_v[i])
        h = _ple(h, p_all[i], ln_ple[i], ple_w_proj[i], ple_w_gate[i])
    hp, hs = split(h, (D_MODEL,))
    return (hp, hs, jnp.stack(ssm_p), jnp.stack(conv_p), ckv_p, kr_p,
            jnp.stack(ssm_s), jnp.stack(conv_s), ckv_s, kr_s)
```
